```python
import jax
import jax.numpy as jnp
from jax import lax
import numpy as np

D_MODEL = 1024
BATCH = 8
SEQ = 4096
DEPTH = 2

HEAD_DIM = 64
FOX_HEADS = 6
NSA_HEADS = 4
MOBA_HEADS = 6
FOX_W = FOX_HEADS * HEAD_DIM
NSA_W = NSA_HEADS * HEAD_DIM
MOBA_W = MOBA_HEADS * HEAD_DIM
N_BRANCHES = 3

Q_BLOCK = 128
NSA_Q_BLOCK = 64
MOBA_Q_BLOCK = 32

NSA_CMP_LEN = 32
NSA_CMP_STRIDE = 16
NSA_CMP_HIDDEN = 256
NSA_SEL_LEN = 64
NSA_TOPN = 16
NSA_LOCAL = 2
NSA_WINDOW = 512

MOBA_BLOCK = 256
MOBA_TOPK = 3

N_GROUPS = 4
EXPERTS_PER_GROUP = 4
N_EXPERTS = N_GROUPS * EXPERTS_PER_GROUP
TOPK_IN_GROUP = 2
EXPERT_HIDDEN = 256

RMS_EPS = 1e-6
NEG_INF = -1e30
FORCE_SCORE = 1e9
FORGET_BIAS_SHIFT = 3.0

IN_SPLITS = (FOX_W, FOX_W, FOX_W, FOX_HEADS,
             NSA_W, HEAD_DIM, HEAD_DIM, HEAD_DIM, HEAD_DIM, HEAD_DIM, HEAD_DIM, N_BRANCHES * NSA_HEADS,
             MOBA_W, MOBA_W, MOBA_W,
             N_BRANCHES * D_MODEL)
IN_COLS = sum(IN_SPLITS)
FORGET_COL = 3 * FOX_W

kernel_name = 'hybrid_fox_nsa_moba_hmoe'


def _rmsnorm(x, g):
    xf = x.astype(jnp.float32)
    y = xf * lax.rsqrt(jnp.mean(xf * xf, axis=-1, keepdims=True) + RMS_EPS)
    return (y * g.astype(jnp.float32)).astype(x.dtype)


def _heads(t, n):
    b, s, _ = t.shape
    return t.reshape(b, s, n, HEAD_DIM).transpose(0, 2, 1, 3)


def _merge(o):
    b, h, s, d = o.shape
    return o.transpose(0, 2, 1, 3).reshape(b, s, h * d)


def _alibi_slopes(n):
    return jnp.asarray(2.0 ** (-8.0 * np.arange(1, n + 1) / n), dtype=jnp.float32)


def _masked_softmax(logits, mask):
    p = jax.nn.softmax(jnp.where(mask, logits, NEG_INF), axis=-1)
    return jnp.where(mask, p, 0.0)


def _fox_attention(q, k, v, forget_logit):
    b, h, s, d = q.shape
    nqb = s // Q_BLOCK
    scale = d ** -0.5
    c = jnp.cumsum(jax.nn.log_sigmoid(forget_logit.astype(jnp.float32)), axis=1).transpose(0, 2, 1)
    qb = q.reshape(b, h, nqb, Q_BLOCK, d).transpose(2, 0, 1, 3, 4)
    cb = c.reshape(b, h, nqb, Q_BLOCK).transpose(2, 0, 1, 3)
    key_pos = jnp.arange(s)

    def one_block(args):
        i, q_i, c_i = args
        t = i * Q_BLOCK + jnp.arange(Q_BLOCK)
        logits = jnp.einsum('bhqd,bhkd->bhqk', q_i, k).astype(jnp.float32) * scale
        logits = logits + c_i[..., :, None] - c[:, :, None, :]
        p = _masked_softmax(logits, key_pos[None, :] <= t[:, None])
        return jnp.einsum('bhqk,bhkd->bhqd', p.astype(v.dtype), v)

    o = lax.map(one_block, (jnp.arange(nqb), qb, cb))
    return o.transpose(1, 2, 0, 3, 4).reshape(b, h, s, d)


def _nsa_compress(kv, w1, b1, w2, pos):
    s = kv.shape[1]
    nc = (s - NSA_CMP_LEN) // NSA_CMP_STRIDE + 1
    idx = jnp.arange(nc)[:, None] * NSA_CMP_STRIDE + jnp.arange(NSA_CMP_LEN)[None, :]
    blocks = kv[:, idx, :] + pos
    hid = jax.nn.gelu(jnp.einsum('bnld,ldh->bnh', blocks, w1) + b1)
    return jnp.einsum('bnh,hd->bnd', hid, w2)


def _nsa_compressed_and_selected(q, k_cmp, v_cmp, k_sel, v_sel, slopes):
    b, h, s, d = q.shape
    scale = d ** -0.5
    nc = k_cmp.shape[1]
    n_sel = s // NSA_SEL_LEN
    topn = min(NSA_TOPN, n_sel)
    ratio = NSA_SEL_LEN // NSA_CMP_STRIDE
    span = NSA_CMP_LEN // NSA_CMP_STRIDE
    imp_len = ratio * n_sel + span - 1
    offsets = [m + n for m in range(ratio) for n in range(span)]
    cmp_end = jnp.arange(nc) * NSA_CMP_STRIDE + NSA_CMP_LEN - 1
    k_blk = k_sel.reshape(b, n_sel, NSA_SEL_LEN, d)
    v_blk = v_sel.reshape(b, n_sel, NSA_SEL_LEN, d)
    blk = jnp.arange(n_sel)
    b_idx = jnp.arange(b)[:, None, None]
    n_keys = topn * NSA_SEL_LEN
    nqb = s // NSA_Q_BLOCK
    qb = q.reshape(b, h, nqb, NSA_Q_BLOCK, d).transpose(2, 0, 1, 3, 4)

    def one_block(args):
        i, q_i = args
        t = i * NSA_Q_BLOCK + jnp.arange(NSA_Q_BLOCK)
        dist_c = t[:, None] - cmp_end[None, :]
        logits_c = (jnp.einsum('bhqd,bnd->bhqn', q_i, k_cmp).astype(jnp.float32) * scale
                    - slopes[:, None, None] * dist_c)
        p_c = _masked_softmax(logits_c, dist_c >= 0)
        o_c = jnp.einsum('bhqn,bnd->bhqd', p_c.astype(v_cmp.dtype), v_cmp)
        imp = jnp.pad(p_c.sum(axis=1), ((0, 0), (0, 0), (0, imp_len - nc)))
        imp_sel = sum(imp[..., o: o + ratio * n_sel: ratio] for o in offsets)
        cur = t // NSA_SEL_LEN
        causal = blk[None, :] <= cur[:, None]
        forced = (blk[None, :] == 0) | (blk[None, :] > cur[:, None] - NSA_LOCAL)
        score = jnp.where(causal, jnp.where(forced, FORCE_SCORE, imp_sel), -FORCE_SCORE)
        _, sel = lax.top_k(score, topn)
        k_g = k_blk[b_idx, sel].reshape(b, NSA_Q_BLOCK, n_keys, d)
        v_g = v_blk[b_idx, sel].reshape(b, NSA_Q_BLOCK, n_keys, d)
        key_pos = (sel[..., None] * NSA_SEL_LEN + jnp.arange(NSA_SEL_LEN)).reshape(b, NSA_Q_BLOCK, n_keys)
        dist_s = t[None, :, None] - key_pos
        logits_s = (jnp.einsum('bhqd,bqkd->bhqk', q_i, k_g).astype(jnp.float32) * scale
                    - slopes[None, :, None, None] * dist_s[:, None])
        p_s = _masked_softmax(logits_s, (dist_s >= 0)[:, None])
        o_s = jnp.einsum('bhqk,bqkd->bhqd', p_s.astype(v_g.dtype), v_g)
        return o_c, o_s

    o_c, o_s = lax.map(one_block, (jnp.arange(nqb), qb))
    o_c = o_c.transpose(1, 2, 0, 3, 4).reshape(b, h, s, d)
    o_s = o_s.transpose(1, 2, 0, 3, 4).reshape(b, h, s, d)
    return o_c, o_s


def _nsa_window(q, k, v, slopes):
    b, h, s, d = q.shape
    scale = d ** -0.5
    nqb = s // Q_BLOCK
    nprev = NSA_WINDOW // Q_BLOCK

    def band(t):
        tb = jnp.pad(t.reshape(b, nqb, Q_BLOCK, d), ((0, 0), (nprev, 0), (0, 0), (0, 0)))
        return jnp.concatenate([tb[:, j: j + nqb] for j in range(nprev + 1)], axis=2)

    k_band, v_band = band(k), band(v)
    q_pos = jnp.arange(nqb)[:, None] * Q_BLOCK + jnp.arange(Q_BLOCK)[None, :]
    key_pos = (jnp.arange(nqb)[:, None] - nprev) * Q_BLOCK + jnp.arange((nprev + 1) * Q_BLOCK)[None, :]
    diff = q_pos[:, :, None] - key_pos[:, None, :]
    mask = (diff >= 0) & (diff < NSA_WINDOW) & (key_pos[:, None, :] >= 0)
    qb = q.reshape(b, h, nqb, Q_BLOCK, d)
    logits = (jnp.einsum('bhnqd,bnkd->bhnqk', qb, k_band).astype(jnp.float32) * scale
              - slopes[:, None, None, None] * diff)
    p = _masked_softmax(logits, mask)
    o = jnp.einsum('bhnqk,bnkd->bhnqd', p.astype(v.dtype), v_band)
    return o.reshape(b, h, s, d)


def _moba_attention(q, k, v, slopes):
    b, h, s, d = q.shape
    scale = d ** -0.5
    nb = max(-(-s // MOBA_BLOCK), MOBA_TOPK + 1)
    pad = ((0, 0), (0, 0), (0, nb * MOBA_BLOCK - s), (0, 0))
    kp, vp = jnp.pad(k, pad), jnp.pad(v, pad)
    k_blk = kp.reshape(b, h, nb, MOBA_BLOCK, d)
    v_blk = vp.reshape(b, h, nb, MOBA_BLOCK, d)
    k_mean = k_blk.mean(axis=3)
    blk = jnp.arange(nb)
    b_idx = jnp.arange(b)[:, None, None, None]
    h_idx = jnp.arange(h)[None, :, None, None]
    n_keys = MOBA_TOPK * MOBA_BLOCK
    nqb = s // MOBA_Q_BLOCK
    qb = q.reshape(b, h, nqb, MOBA_Q_BLOCK, d).transpose(2, 0, 1, 3, 4)

    def one_block(args):
        i, q_i = args
        t = i * MOBA_Q_BLOCK + jnp.arange(MOBA_Q_BLOCK)
        cur = (i * MOBA_Q_BLOCK) // MOBA_BLOCK
        gate = jnp.einsum('bhqd,bhnd->bhqn', q_i, k_mean).astype(jnp.float32)
        gate = jnp.where(blk < cur, gate, -FORCE_SCORE)
        _, sel = lax.top_k(gate, MOBA_TOPK)
        k_g = k_blk[b_idx, h_idx, sel].reshape(b, h, MOBA_Q_BLOCK, n_keys, d)
        v_g = v_blk[b_idx, h_idx, sel].reshape(b, h, MOBA_Q_BLOCK, n_keys, d)
        key_pos = (sel[..., None] * MOBA_BLOCK + jnp.arange(MOBA_BLOCK)).reshape(b, h, MOBA_Q_BLOCK, n_keys)
        mask_past = jnp.repeat(sel < cur, MOBA_BLOCK, axis=-1)
        logits_past = (jnp.einsum('bhqd,bhqkd->bhqk', q_i, k_g).astype(jnp.float32) * scale
                       - slopes[None, :, None, None] * (t[None, None, :, None] - key_pos))
        own_k = lax.dynamic_slice_in_dim(kp, cur * MOBA_BLOCK, MOBA_BLOCK, axis=2)
        own_v = lax.dynamic_slice_in_dim(vp, cur * MOBA_BLOCK, MOBA_BLOCK, axis=2)
        dist_own = t[:, None] - (cur * MOBA_BLOCK + jnp.arange(MOBA_BLOCK))[None, :]
        logits_own = (jnp.einsum('bhqd,bhkd->bhqk', q_i, own_k).astype(jnp.float32) * scale
                      - slopes[:, None, None] * dist_own)
        mask_own = jnp.broadcast_to(dist_own >= 0, logits_own.shape)
        p = _masked_softmax(jnp.concatenate([logits_past, logits_own], axis=-1),
                            jnp.concatenate([mask_past, mask_own], axis=-1)).astype(v.dtype)
        return (jnp.einsum('bhqk,bhqkd->bhqd', p[..., :n_keys], v_g)
                + jnp.einsum('bhqk,bhkd->bhqd', p[..., n_keys:], own_v))

    o = lax.map(one_block, (jnp.arange(nqb), qb))
    return o.transpose(1, 2, 0, 3, 4).reshape(b, h, s, d)


def _token_mixers(h, w_in, b_in, cmp_w1, cmp_b1, cmp_w2, cmp_pos, w_br_fox, w_br_nsa, w_br_moba, w_out):
    b, s, _ = h.shape
    proj = jnp.einsum('bsd,dc->bsc', h, w_in) + b_in
    split_points = np.cumsum(IN_SPLITS)[:-1].tolist()
    (fq, fk, fv, f_forget, nq, kc, vc, ks, vs, kw, vw, n_gate,
     mq, mk, mv, merge_logit) = jnp.split(proj, split_points, axis=-1)
    o_fox = _fox_attention(_heads(fq, FOX_HEADS), _heads(fk, FOX_HEADS), _heads(fv, FOX_HEADS), f_forget)
    slopes_nsa = _alibi_slopes(NSA_HEADS)
    q_nsa = _heads(nq, NSA_HEADS)
    k_cmp = _nsa_compress(kc, cmp_w1[0], cmp_b1[0], cmp_w2[0], cmp_pos[0])
    v_cmp = _nsa_compress(vc, cmp_w1[1], cmp_b1[1], cmp_w2[1], cmp_pos[1])
    o_c, o_s = _nsa_compressed_and_selected(q_nsa, k_cmp, v_cmp, ks, vs, slopes_nsa)
    o_w = _nsa_window(q_nsa, kw, vw, slopes_nsa)
    g_nsa = jax.nn.sigmoid(n_gate.astype(jnp.float32)).reshape(b, s, N_BRANCHES, NSA_HEADS).transpose(2, 0, 3, 1)[..., None]
    o_nsa = (g_nsa[0] * o_c + g_nsa[1] * o_s + g_nsa[2] * o_w).astype(h.dtype)
    o_moba = _moba_attention(_heads(mq, MOBA_HEADS), _heads(mk, MOBA_HEADS), _heads(mv, MOBA_HEADS),
                             _alibi_slopes(MOBA_HEADS))
    y_fox = jnp.einsum('bsk,kd->bsd', _merge(o_fox), w_br_fox)
    y_nsa = jnp.einsum('bsk,kd->bsd', _merge(o_nsa), w_br_nsa)
    y_moba = jnp.einsum('bsk,kd->bsd', _merge(o_moba), w_br_moba)
    gate = jax.nn.sigmoid(merge_logit.astype(jnp.float32)).reshape(b, s, N_BRANCHES, D_MODEL)
    merged = (gate[:, :, 0] * y_fox + gate[:, :, 1] * y_nsa + gate[:, :, 2] * y_moba).astype(h.dtype)
    return jnp.einsum('bsd,de->bse', merged, w_out)


def _hier_moe(h, w_route_grp, b_route_grp, w_route_exp, b_route_exp, w_exp_gate, w_exp_up, w_exp_down):
    b, s, d = h.shape
    xt = h.reshape(b * s, d)
    grp_logits = (xt @ w_route_grp + b_route_grp).astype(jnp.float32)
    grp_prob = jax.nn.softmax(grp_logits, axis=-1)
    grp_weight = jax.nn.one_hot(jnp.argmax(grp_logits, axis=-1), N_GROUPS, dtype=jnp.float32) * grp_prob
    exp_logits = (xt @ w_route_exp + b_route_exp).astype(jnp.float32).reshape(-1, N_GROUPS, EXPERTS_PER_GROUP)
    exp_prob = jax.nn.softmax(exp_logits, axis=-1)
    top_p, top_i = lax.top_k(exp_prob, TOPK_IN_GROUP)
    top_p = top_p / top_p.sum(axis=-1, keepdims=True)
    within = jnp.sum(jax.nn.one_hot(top_i, EXPERTS_PER_GROUP, dtype=jnp.float32) * top_p[..., None], axis=-2)
    combine = grp_weight[:, :, None] * within
    out = jnp.zeros_like(xt)
    for g in range(N_GROUPS):
        e = slice(g * EXPERTS_PER_GROUP, (g + 1) * EXPERTS_PER_GROUP)
        a = jnp.einsum('td,edh->teh', xt, w_exp_gate[e])
        u = jnp.einsum('td,edh->teh', xt, w_exp_up[e])
        hid = (jax.nn.silu(a) * u * combine[:, g, :, None]).astype(xt.dtype)
        out = out + jnp.einsum('teh,ehd->td', hid, w_exp_down[e])
    return out.reshape(b, s, d)


def setup_inputs(seed: int = 0) -> dict:
    key = jax.random.key(seed)
    ks = jax.random.split(key, 21)
    f32 = jnp.float32
    L = DEPTH

    def nrm(k, shape, scale):
        return jax.random.normal(k, shape, f32) * scale

    b_in = nrm(ks[5], (L, IN_COLS), 0.02)
    b_in = b_in.at[:, FORGET_COL:FORGET_COL + FOX_HEADS].add(FORGET_BIAS_SHIFT)
    return {
        'x': nrm(ks[0], (BATCH, SEQ, D_MODEL), 1.0),
        'norm_mix_g': 1.0 + nrm(ks[1], (L, D_MODEL), 0.02),
        'norm_ffn_g': 1.0 + nrm(ks[2], (L, D_MODEL), 0.02),
        'final_norm_g': 1.0 + nrm(ks[3], (D_MODEL,), 0.02),
        'w_in': nrm(ks[4], (L, D_MODEL, IN_COLS), D_MODEL ** -0.5),
        'b_in': b_in,
        'cmp_w1': nrm(ks[6], (L, 2, NSA_CMP_LEN, HEAD_DIM, NSA_CMP_HIDDEN), (NSA_CMP_LEN * HEAD_DIM) ** -0.5),
        'cmp_b1': nrm(ks[7], (L, 2, NSA_CMP_HIDDEN), 0.02),
        'cmp_w2': nrm(ks[8], (L, 2, NSA_CMP_HIDDEN, HEAD_DIM), NSA_CMP_HIDDEN ** -0.5),
        'cmp_pos': nrm(ks[9], (L, 2, NSA_CMP_LEN, HEAD_DIM), 0.02),
        'w_br_fox': nrm(ks[10], (L, FOX_W, D_MODEL), FOX_W ** -0.5),
        'w_br_nsa': nrm(ks[11], (L, NSA_W, D_MODEL), NSA_W ** -0.5),
        'w_br_moba': nrm(ks[12], (L, MOBA_W, D_MODEL), MOBA_W ** -0.5),
        'w_out': nrm(ks[13], (L, D_MODEL, D_MODEL), D_MODEL ** -0.5),
        'w_route_grp': nrm(ks[14], (L, D_MODEL, N_GROUPS), D_MODEL ** -0.5),
        'b_route_grp': nrm(ks[15], (L, N_GROUPS), 0.01),
        'w_route_exp': nrm(ks[16], (L, D_MODEL, N_EXPERTS), D_MODEL ** -0.5),
        'b_route_exp': nrm(ks[17], (L, N_EXPERTS), 0.01),
        'w_exp_gate': nrm(ks[18], (L, N_EXPERTS, D_MODEL, EXPERT_HIDDEN), D_MODEL ** -0.5),
        'w_exp_up': nrm(ks[19], (L, N_EXPERTS, D_MODEL, EXPERT_HIDDEN), D_MODEL ** -0.5),
        'w_exp_down': nrm(ks[20], (L, N_EXPERTS, EXPERT_HIDDEN, D_MODEL), EXPERT_HIDDEN ** -0.5),
    }


def reference(x, norm_mix_g, norm_ffn_g, final_norm_g, w_in, b_in, cmp_w1, cmp_b1, cmp_w2, cmp_pos,
              w_br_fox, w_br_nsa, w_br_moba, w_out, w_route_grp, b_route_grp, w_route_exp, b_route_exp,
              w_exp_gate, w_exp_up, w_exp_down):
    for l in range(DEPTH):
        h = _rmsnorm(x, norm_mix_g[l])
        x = x + _token_mixers(h, w_in[l], b_in[l], cmp_w1[l], cmp_b1[l], cmp_w2[l], cmp_pos[l],
                              w_br_fox[l], w_br_nsa[l], w_br_moba[l], w_out[l])
        h = _rmsnorm(x, norm_ffn_g[l])
        x = x + _hier_moe(h, w_route_grp[l], b_route_grp[l], w_route_exp[l], b_route_exp[l],
                          w_exp_gate[l], w_exp_up[l], w_exp_down[l])
    return _rmsnorm(x, final_norm_g)
```

```python
import functools

import numpy as np
import jax
import jax.numpy as jnp
from jax import lax
from jax.experimental import pallas as pl
from jax.experimental.pallas import tpu as pltpu

F32 = jnp.float32
BF16 = jnp.bfloat16
HIGHEST = lax.Precision.HIGHEST

D_MODEL = 1024
HEAD_DIM = 64
FOX_HEADS = 6
NSA_HEADS = 4
MOBA_HEADS = 6
FOX_W = FOX_HEADS * HEAD_DIM
NSA_W = NSA_HEADS * HEAD_DIM
MOBA_W = MOBA_HEADS * HEAD_DIM
N_BRANCHES = 3

NSA_CMP_LEN = 32
NSA_CMP_STRIDE = 16
NSA_CMP_HIDDEN = 256
NSA_SEL_LEN = 64
NSA_TOPN = 16
NSA_LOCAL = 2
NSA_WINDOW = 512
MOBA_BLOCK = 256
MOBA_TOPK = 3

N_GROUPS = 4
EXPERTS_PER_GROUP = 4
N_EXPERTS = N_GROUPS * EXPERTS_PER_GROUP
TOPK_IN_GROUP = 2
EXPERT_HIDDEN = 256

RMS_EPS = 1e-6
NEG_INF = -1e30
FORCE_SCORE = 1e9

LANES = 128
TL = 256
KAUG = 2 * LANES
VMEM_LIMIT = 56 * 1024 * 1024

_O_FQ = 0
_O_FK = _O_FQ + FOX_W
_O_FV = _O_FK + FOX_W
_O_FF = _O_FV + FOX_W
_O_NQ = _O_FF + FOX_HEADS
_O_KC = _O_NQ + NSA_W
_O_VC = _O_KC + HEAD_DIM
_O_KS = _O_VC + HEAD_DIM
_O_VS = _O_KS + HEAD_DIM
_O_KW = _O_VS + HEAD_DIM
_O_VW = _O_KW + HEAD_DIM
_O_NG = _O_VW + HEAD_DIM
_O_MQ = _O_NG + N_BRANCHES * NSA_HEADS
_O_MK = _O_MQ + MOBA_W
_O_MV = _O_MK + MOBA_W
_O_ML = _O_MV + MOBA_W
_O_END = _O_ML + N_BRANCHES * D_MODEL

_S_FK = 0
_S_MK = _S_FK + FOX_HEADS * LANES
_S_KS = _S_MK + MOBA_HEADS * LANES
_S_KW = _S_KS + LANES
_S_KC = _S_KW + LANES
_S_VC = _S_KC + LANES
_S_FF = _S_VC + LANES
_S_END = _S_FF + LANES
_T_FQ = 0
_T_FV = _T_FQ + FOX_W
_T_MQ = _T_FV + FOX_W
_T_MV = _T_MQ + MOBA_W
_T_NQ = _T_MV + MOBA_W
_T_NV = _T_NQ + NSA_W
_T_NG = _T_NV + 2 * HEAD_DIM
_T_END = _T_NG + 16


def _alibi_slopes(n):
    return 2.0 ** (-8.0 * np.arange(1, n + 1) / n)


def _split3(v):
    hi = v.astype(BF16)
    r1 = v - hi.astype(F32)
    lo = r1.astype(BF16)
    r2 = r1 - lo.astype(F32)
    return hi, lo, r2.astype(BF16)


def _nt_dot(a, b, precision=None):
    return lax.dot_general(a, b, (((1,), (1,)), ((), ())), preferred_element_type=F32, precision=precision)


def _dot(a, b, precision=None):
    return jnp.dot(a, b, preferred_element_type=F32, precision=precision)


def _rms(x, g):
    return x * lax.rsqrt(jnp.mean(x * x, axis=-1, keepdims=True) + RMS_EPS) * g


def _proj_body(x_ref, g_ref, ws_ref, bs_ref, wt_ref, bt_ref,
               fk_ref, mk_ref, ks_ref, kw_ref, kc_ref, vc_ref, negc_ref, kmean_ref,
               fqT_ref, fvT_ref, mqT_ref, mvT_ref, nqT_ref, nvT_ref, gT_ref, carry_ref, *, tm):
    i = pl.program_id(1)
    hb = _rms(x_ref[0], g_ref[...]).astype(BF16)

    def sdot(a, b):
        return _dot(hb, ws_ref[:, a:b]) + bs_ref[:, a:b]

    def tdot(a, b):
        return _nt_dot(wt_ref[a:b, :], hb) + bt_ref[a:b, :]

    def put_t(ref, val):
        for c in range(tm // TL):
            ref[0, c] = val[:, c * TL:(c + 1) * TL]

    fk_ref[0] = sdot(_S_FK, _S_MK).astype(BF16)
    mk = sdot(_S_MK, _S_KS)
    mk_ref[0] = mk.astype(BF16)
    nblk = tm // MOBA_BLOCK
    means = [jnp.mean(mk[c * MOBA_BLOCK:(c + 1) * MOBA_BLOCK, :], axis=0, keepdims=True) for c in range(nblk)]
    kmean_ref[0, 0] = jnp.concatenate(means + [jnp.zeros((8 - nblk, mk.shape[1]), F32)], axis=0)
    ks_ref[0] = sdot(_S_KS, _S_KW).astype(BF16)
    kw_ref[0] = sdot(_S_KW, _S_KC).astype(BF16)
    kc_ref[0] = sdot(_S_KC, _S_VC)
    vc_ref[0] = sdot(_S_VC, _S_FF)

    @pl.when(i == 0)
    def _():
        carry_ref[...] = jnp.zeros_like(carry_ref)

    fl = sdot(_S_FF, _S_END)
    ls = jnp.minimum(fl, 0.0) - jnp.log1p(jnp.exp(-jnp.abs(fl)))
    ls = jnp.where(lax.broadcasted_iota(jnp.int32, ls.shape, 1) < FOX_HEADS, ls, 0.0)
    tri = (lax.broadcasted_iota(jnp.int32, (tm, tm), 1) <= lax.broadcasted_iota(jnp.int32, (tm, tm), 0)).astype(F32)
    cs = _dot(tri, ls, precision=HIGHEST) + carry_ref[0:1, :]
    negc_ref[0] = -cs
    carry_ref[0:1, :] = cs[tm - 1:tm, :]

    put_t(fqT_ref, tdot(_T_FQ, _T_FV).astype(BF16))
    put_t(fvT_ref, tdot(_T_FV, _T_MQ).astype(BF16))
    put_t(mqT_ref, tdot(_T_MQ, _T_MV).astype(BF16))
    put_t(mvT_ref, tdot(_T_MV, _T_NQ).astype(BF16))
    put_t(nqT_ref, tdot(_T_NQ, _T_NV).astype(BF16))
    put_t(nvT_ref, tdot(_T_NV, _T_NG).astype(BF16))
    put_t(gT_ref, jax.nn.sigmoid(tdot(_T_NG, _T_END)))


def _proj_weights(w, b):
    def pad_heads(cols, nh):
        m = cols.reshape(cols.shape[0], nh, HEAD_DIM)
        return jnp.pad(m, ((0, 0), (0, 0), (0, LANES - HEAD_DIM))).reshape(cols.shape[0], nh * LANES)

    def pad_to(cols, n):
        return jnp.pad(cols, ((0, 0), (0, n - cols.shape[1])))

    wb = jnp.concatenate([w, b[None, :]], axis=0)
    sl = lambda o, n: wb[:, o:o + n]
    std = jnp.concatenate([
        pad_heads(sl(_O_FK, FOX_W), FOX_HEADS), pad_heads(sl(_O_MK, MOBA_W), MOBA_HEADS),
        pad_to(sl(_O_KS, HEAD_DIM), LANES), pad_to(sl(_O_KW, HEAD_DIM), LANES),
        pad_to(sl(_O_KC, HEAD_DIM), LANES), pad_to(sl(_O_VC, HEAD_DIM), LANES),
        pad_to(sl(_O_FF, FOX_HEADS), LANES)], axis=1)
    scale = HEAD_DIM ** -0.5
    tr = jnp.concatenate([
        sl(_O_FQ, FOX_W) * scale, sl(_O_FV, FOX_W), sl(_O_MQ, MOBA_W) * scale, sl(_O_MV, MOBA_W),
        sl(_O_NQ, NSA_W) * scale, sl(_O_VS, HEAD_DIM), sl(_O_VW, HEAD_DIM),
        pad_to(sl(_O_NG, N_BRANCHES * NSA_HEADS), 16)], axis=1)
    ws, bs = std[:-1].astype(BF16), std[-1:].astype(F32)
    wt, bt = tr[:-1].T.astype(BF16), tr[-1:].T.astype(F32)
    return ws, bs, wt, bt


def _project(x, g, ws, bs, wt, bt, tm=512):
    B, S, D = x.shape
    nT = S // TL
    nt = tm // TL
    grid = (B, S // tm)
    row = lambda n, dt: jax.ShapeDtypeStruct((B, S, n), dt)
    tiled = lambda n, dt: jax.ShapeDtypeStruct((B, nT, n, TL), dt)
    row_spec = lambda n: pl.BlockSpec((1, tm, n), lambda b, i: (b, i, 0))
    tiled_spec = lambda n: pl.BlockSpec((1, nt, n, TL), lambda b, i: (b, i, 0, 0))
    full = lambda a: pl.BlockSpec(a.shape, lambda b, i: (0,) * a.ndim)
    out_shape = [row(FOX_HEADS * LANES, BF16), row(MOBA_HEADS * LANES, BF16), row(LANES, BF16), row(LANES, BF16),
                 row(LANES, F32), row(LANES, F32), row(LANES, F32),
                 jax.ShapeDtypeStruct((B, S // tm, 8, MOBA_HEADS * LANES), F32),
                 tiled(FOX_W, BF16), tiled(FOX_W, BF16), tiled(MOBA_W, BF16), tiled(MOBA_W, BF16),
                 tiled(NSA_W, BF16), tiled(2 * HEAD_DIM, BF16), tiled(16, F32)]
    out_specs = [row_spec(FOX_HEADS * LANES), row_spec(MOBA_HEADS * LANES), row_spec(LANES), row_spec(LANES),
                 row_spec(LANES), row_spec(LANES), row_spec(LANES),
                 pl.BlockSpec((1, 1, 8, MOBA_HEADS * LANES), lambda b, i: (b, i, 0, 0)),
                 tiled_spec(FOX_W), tiled_spec(FOX_W), tiled_spec(MOBA_W), tiled_spec(MOBA_W),
                 tiled_spec(NSA_W), tiled_spec(2 * HEAD_DIM), tiled_spec(16)]
    outs = pl.pallas_call(
        functools.partial(_proj_body, tm=tm),
        grid=grid,
        in_specs=[pl.BlockSpec((1, tm, D), lambda b, i: (b, i, 0)), full(g), full(ws), full(bs), full(wt), full(bt)],
        out_specs=out_specs,
        out_shape=out_shape,
        scratch_shapes=[pltpu.VMEM((8, LANES), F32)],
        compiler_params=pltpu.CompilerParams(dimension_semantics=("arbitrary", "arbitrary"),
                                             vmem_limit_bytes=VMEM_LIMIT),
        name="proj",
    )(x, g, ws, bs, wt, bt)
    (fk, mk, ks, kw, kc, vc, negc, kmean8, fqT, fvT, mqT, mvT, nqT, nvT, gT) = outs
    nblk = tm // MOBA_BLOCK
    kmean = kmean8[:, :, :nblk, :].reshape(B, (S // tm) * nblk, MOBA_HEADS * LANES)
    return dict(fk=fk, mk=mk, ks=ks, kw=kw, kc=kc, vc=vc, negc=negc, kmean=kmean,
                fqT=fqT, fvT=fvT, mqT=mqT, mvT=mvT, nqT=nqT, nvT=nvT, gT=gT)


def _flash_update(sT, vT, m, l, acc):
    m_new = jnp.maximum(m, jnp.max(sT, axis=0, keepdims=True))
    p = jnp.exp(sT - m_new)
    alpha = jnp.exp(m - m_new)
    l_new = alpha * l + jnp.sum(p, axis=0, keepdims=True)
    acc_new = alpha * acc + _dot(vT, p.astype(BF16))
    return m_new, l_new, acc_new


def _tile_iotas(tk, tq):
    kk = lax.broadcasted_iota(jnp.int32, (tk, tq), 0)
    qq = lax.broadcasted_iota(jnp.int32, (tk, tq), 1)
    return kk, qq


def _feature_rows(nrows, tq, value):
    r = lax.broadcasted_iota(jnp.int32, (16, tq), 0)
    return jnp.where(r < nrows, value, 0.0).astype(F32)


def _fox_body(qT_ref, k_ref, vT_ref, negc_ref, e_ref, o_ref, kfeat_ref, *, S):
    i = pl.program_id(2)

    @pl.when(i == 0)
    def _():
        chunk = 512
        for c in range(S // chunk):
            hi, lo, lo2 = _split3(negc_ref[0, c * chunk:(c + 1) * chunk, :])
            feat = _dot(hi, e_ref[0, 0]) + _dot(lo, e_ref[0, 1]) + _dot(lo2, e_ref[0, 2])
            kfeat_ref[c * chunk:(c + 1) * chunk, :] = feat.astype(BF16)

    qT = qT_ref[0, 0]
    tq = qT.shape[1]
    feat_rows = jnp.concatenate([_feature_rows(3, tq, 1.0), jnp.zeros((LANES - 16, tq), F32)], axis=0)
    q2T = jnp.concatenate([qT, jnp.zeros((HEAD_DIM, tq), BF16), feat_rows.astype(BF16)], axis=0)

    def scores(j):
        k2 = jnp.concatenate([k_ref[0, pl.ds(j * TL, TL), :], kfeat_ref[pl.ds(j * TL, TL), :]], axis=1)
        return _dot(k2, q2T)

    def step(j, carry):
        return _flash_update(scores(j), vT_ref[0, j], *carry)

    init = (jnp.full((1, tq), NEG_INF, F32), jnp.zeros((1, tq), F32), jnp.zeros((HEAD_DIM, tq), F32))
    carry = lax.fori_loop(0, i, step, init)
    kk, qq = _tile_iotas(TL, tq)
    sT = jnp.where(kk <= qq, scores(i), NEG_INF)
    m, l, acc = _flash_update(sT, vT_ref[0, i], *carry)
    o_ref[0, 0] = (acc / l).astype(BF16)


def _fox_attention(p, e_fox):
    B, nT = p["fqT"].shape[:2]
    S = nT * TL
    return pl.pallas_call(
        functools.partial(_fox_body, S=S),
        grid=(B, FOX_HEADS, nT),
        in_specs=[pl.BlockSpec((1, 1, HEAD_DIM, TL), lambda b, h, i: (b, i, h, 0)),
                  pl.BlockSpec((1, S, LANES), lambda b, h, i: (b, 0, h)),
                  pl.BlockSpec((1, nT, HEAD_DIM, TL), lambda b, h, i: (b, 0, h, 0)),
                  pl.BlockSpec((1, S, LANES), lambda b, h, i: (b, 0, 0)),
                  pl.BlockSpec((1, 3, LANES, LANES), lambda b, h, i: (h, 0, 0, 0))],
        out_specs=pl.BlockSpec((1, 1, HEAD_DIM, TL), lambda b, h, i: (b, i, h, 0)),
        out_shape=jax.ShapeDtypeStruct((B, nT, FOX_W, TL), BF16),
        scratch_shapes=[pltpu.VMEM((S, LANES), BF16)],
        compiler_params=pltpu.CompilerParams(dimension_semantics=("arbitrary",) * 3, vmem_limit_bytes=VMEM_LIMIT),
        name="fox_attn",
    )(p["fqT"], p["fk"], p["fvT"], p["negc"], e_fox)


def _rank_rows(score, nrows):
    ridx = lax.broadcasted_iota(jnp.int32, score.shape, 0)
    rank = jnp.zeros(score.shape, F32)
    for n2 in range(nrows):
        row = score[n2:n2 + 1, :]
        ahead = (row > score) | ((row == score) & (n2 < ridx))
        rank = rank + jnp.where(ahead, 1.0, 0.0)
    return rank


def _moba_body(qT_ref, k_ref, vT_ref, kmean_ref, ptab_ref, o_ref, *, nbp):
    i = pl.program_id(2)
    qT = qT_ref[0, 0]
    tq = qT.shape[1]
    zpad = jnp.zeros((HEAD_DIM, tq), BF16)
    gate = _dot(kmean_ref[0].astype(BF16), jnp.concatenate([qT, zpad], axis=0)) * (HEAD_DIM ** 0.5)
    blk = lax.broadcasted_iota(jnp.int32, (nbp, tq), 0)
    gate = jnp.where(blk < i, gate, -FORCE_SCORE)
    sel = (_rank_rows(gate, nbp) < MOBA_TOPK) & (blk < i)
    selbias = jnp.where(sel | (blk == i), 0.0, NEG_INF)
    feat_rows = jnp.concatenate([selbias, _feature_rows(3, tq, 1.0), jnp.zeros((LANES - nbp - 16, tq), F32)], axis=0)
    q2T = jnp.concatenate([qT, zpad, feat_rows.astype(BF16)], axis=0)

    def scores(j):
        k2 = jnp.concatenate([k_ref[0, pl.ds(j * TL, TL), :], ptab_ref[0, pl.ds(j * TL, TL), :]], axis=1)
        return _dot(k2, q2T)

    def step(j, carry):
        return _flash_update(scores(j), vT_ref[0, j], *carry)

    init = (jnp.full((1, tq), NEG_INF, F32), jnp.zeros((1, tq), F32), jnp.zeros((HEAD_DIM, tq), F32))
    carry = lax.fori_loop(0, i, step, init)
    kk, qq = _tile_iotas(TL, tq)
    sT = jnp.where(kk <= qq, scores(i), NEG_INF)
    m, l, acc = _flash_update(sT, vT_ref[0, i], *carry)
    o_ref[0, 0] = (acc / l).astype(BF16)


def _moba_tables(S, nbp):
    pos = np.arange(S)
    onehot = (pos[:, None] // MOBA_BLOCK == np.arange(nbp)[None, :]).astype(np.float32)
    tabs = []
    for slope in _alibi_slopes(MOBA_HEADS):
        hi, lo, lo2 = _split3(jnp.asarray(slope * pos, F32))
        feat = jnp.stack([hi, lo, lo2], axis=1).astype(F32)
        tabs.append(jnp.concatenate([jnp.asarray(onehot), feat, jnp.zeros((S, LANES - nbp - 3), F32)], axis=1))
    return jnp.stack(tabs).astype(BF16)


def _moba_attention(p):
    B, nT = p["mqT"].shape[:2]
    S = nT * TL
    nbp = -(-nT // 16) * 16
    kmean = jnp.pad(p["kmean"], ((0, 0), (0, nbp - nT), (0, 0)))
    ptab = _moba_tables(S, nbp)
    return pl.pallas_call(
        functools.partial(_moba_body, nbp=nbp),
        grid=(B, MOBA_HEADS, nT),
        in_specs=[pl.BlockSpec((1, 1, HEAD_DIM, TL), lambda b, h, i: (b, i, h, 0)),
                  pl.BlockSpec((1, S, LANES), lambda b, h, i: (b, 0, h)),
                  pl.BlockSpec((1, nT, HEAD_DIM, TL), lambda b, h, i: (b, 0, h, 0)),
                  pl.BlockSpec((1, nbp, LANES), lambda b, h, i: (b, 0, h)),
                  pl.BlockSpec((1, S, LANES), lambda b, h, i: (h, 0, 0))],
        out_specs=pl.BlockSpec((1, 1, HEAD_DIM, TL), lambda b, h, i: (b, i, h, 0)),
        out_shape=jax.ShapeDtypeStruct((B, nT, MOBA_W, TL), BF16),
        compiler_params=pltpu.CompilerParams(dimension_semantics=("arbitrary",) * 3, vmem_limit_bytes=VMEM_LIMIT),
        name="moba_attn",
    )(p["mqT"], p["mk"], p["mvT"], kmean, ptab)


def _cmp_body(kc_ref, vc_ref, w1_ref, b1_ref, w2k_ref, w2vT_ref, pos_ref, kcmp_ref, vcmpT_ref, pad_ref, *, S, nc):
    tail = pad_ref.shape[0] - S

    def hidden(src_ref, which):
        pad_ref[0:S, :] = src_ref[0]
        pad_ref[S:S + tail, :] = jnp.zeros((tail, LANES), F32)
        acc = jnp.zeros((nc, NSA_CMP_HIDDEN), F32)
        for l in range(NSA_CMP_LEN):
            rows = pad_ref[pl.ds(l, nc, stride=NSA_CMP_STRIDE), :] + pos_ref[which, l:l + 1, :]
            acc = acc + _dot(rows.astype(BF16), w1_ref[which, l])
        return jax.nn.gelu(acc + b1_ref[which]).astype(BF16)

    kcmp_ref[0] = _dot(hidden(kc_ref, 0), w2k_ref[...]).astype(BF16)
    vcmpT_ref[0] = _nt_dot(w2vT_ref[...], hidden(vc_ref, 1)).astype(BF16)


def _nsa_compress(p, cmp_w1, cmp_b1, cmp_w2, cmp_pos):
    B, S, _ = p["kc"].shape
    nc = S // NSA_CMP_STRIDE
    w1 = jnp.pad(cmp_w1, ((0, 0), (0, 0), (0, LANES - HEAD_DIM), (0, 0))).astype(BF16)
    pos = jnp.pad(cmp_pos, ((0, 0), (0, 0), (0, LANES - HEAD_DIM)))
    b1 = cmp_b1[:, None, :]
    w2k = jnp.pad(cmp_w2[0], ((0, 0), (0, LANES - HEAD_DIM))).astype(BF16)
    w2vT = cmp_w2[1].T.astype(BF16)
    full = lambda a: pl.BlockSpec(a.shape, lambda b: (0,) * a.ndim)
    return pl.pallas_call(
        functools.partial(_cmp_body, S=S, nc=nc),
        grid=(B,),
        in_specs=[pl.BlockSpec((1, S, LANES), lambda b: (b, 0, 0)), pl.BlockSpec((1, S, LANES), lambda b: (b, 0, 0)),
                  full(w1), full(b1), full(w2k), full(w2vT), full(pos)],
        out_specs=[pl.BlockSpec((1, nc, LANES), lambda b: (b, 0, 0)), pl.BlockSpec((1, HEAD_DIM, nc), lambda b: (b, 0, 0))],
        out_shape=[jax.ShapeDtypeStruct((B, nc, LANES), BF16), jax.ShapeDtypeStruct((B, HEAD_DIM, nc), BF16)],
        scratch_shapes=[pltpu.VMEM((S + NSA_CMP_LEN, LANES), F32)],
        compiler_params=pltpu.CompilerParams(dimension_semantics=("arbitrary",), vmem_limit_bytes=VMEM_LIMIT),
        name="nsa_compress",
    )(p["kc"], p["vc"], w1, b1, w2k, w2vT, pos)


def _nsa_body(qT_ref, kcmp_ref, vcmpT_ref, ctab_ref, aT_ref, ks_ref, kw_ref, vT_ref, gT_ref, stab_ref, wtab_ref,
              o_ref, score_ref, acc_ref, m_ref, l_ref, *, nc, nsel):
    i = pl.program_id(1)
    tq = TL
    t0 = i * tq
    slopes = _alibi_slopes(NSA_HEADS)
    zpad = jnp.zeros((HEAD_DIM, tq), BF16)

    def q2T_of(h, selbias):
        rows = jnp.concatenate([selbias, _feature_rows(3, tq, float(slopes[h])),
                                jnp.zeros((LANES - nsel - 16, tq), F32)], axis=0)
        return jnp.concatenate([qT_ref[0, 0, h * HEAD_DIM:(h + 1) * HEAD_DIM, :], zpad, rows.astype(BF16)], axis=0)

    kc2 = jnp.concatenate([kcmp_ref[0], ctab_ref[...]], axis=1)
    nn, qq = _tile_iotas(nc, tq)
    vis = nn * NSA_CMP_STRIDE + (NSA_CMP_LEN - 1) <= t0 + qq
    nosel = jnp.zeros((nsel, tq), F32)
    imp = jnp.zeros((nc, tq), F32)
    o_c = []
    for h in range(NSA_HEADS):
        sT = jnp.where(vis, _dot(kc2, q2T_of(h, nosel)), NEG_INF)
        m = jnp.max(sT, axis=0, keepdims=True)
        e = jnp.where(vis, jnp.exp(sT - m), 0.0)
        l = jnp.sum(e, axis=0, keepdims=True)
        pc = e / jnp.where(l > 0.0, l, 1.0)
        imp = imp + pc
        o_c.append(_dot(vcmpT_ref[0], pc.astype(BF16)))

    imp_sel = _dot(aT_ref[...], imp, precision=HIGHEST)
    jj, qs = _tile_iotas(nsel, tq)
    cur = (t0 + qs) // NSA_SEL_LEN
    causal = jj <= cur
    forced = (jj == 0) | (jj > cur - NSA_LOCAL)
    score = jnp.where(causal, jnp.where(forced, FORCE_SCORE, imp_sel), -FORCE_SCORE)
    score_ref[...] = score

    def rank_step(j2, rank):
        row = score_ref[pl.ds(j2, 1), :]
        ahead = (row > score) | ((row == score) & (j2 < jj))
        return rank + jnp.where(ahead, 1.0, 0.0)

    rank = lax.fori_loop(0, nsel, rank_step, jnp.zeros((nsel, tq), F32))
    selbias = jnp.where(rank < min(NSA_TOPN, nsel), 0.0, NEG_INF)

    kk, qk = _tile_iotas(TL, tq)

    def run_branch(k_ref, tab_ref, v_row0, q2Ts, j_lo, masks):
        for h in range(NSA_HEADS):
            m_ref[h] = jnp.full((8, tq), NEG_INF, F32)
            l_ref[h] = jnp.zeros((8, tq), F32)
            acc_ref[h] = jnp.zeros((HEAD_DIM, tq), F32)

        def tile(j, keep):
            k2 = jnp.concatenate([k_ref[0, pl.ds(j * TL, TL), :], tab_ref[pl.ds(j * TL, TL), :]], axis=1)
            vT = vT_ref[0, j, v_row0:v_row0 + HEAD_DIM, :]
            for h in range(NSA_HEADS):
                sT = _dot(k2, q2Ts[h])
                if keep is not None:
                    sT = jnp.where(keep, sT, NEG_INF)
                m, l, acc = _flash_update(sT, vT, m_ref[h, 0:1, :], l_ref[h, 0:1, :], acc_ref[h])
                m_ref[h, 0:1, :] = m
                l_ref[h, 0:1, :] = l
                acc_ref[h] = acc

        return tile

    q2T_sel = [q2T_of(h, selbias) for h in range(NSA_HEADS)]
    tile = run_branch(ks_ref, stab_ref, 0, q2T_sel, 0, None)

    def sel_step(j, c):
        tile(j, None)
        return c

    lax.fori_loop(0, i, sel_step, 0)
    tile(i, kk <= qk)
    o_s = [acc_ref[h] / l_ref[h, 0:1, :] for h in range(NSA_HEADS)]

    q2T_win = [q2T_of(h, nosel) for h in range(NSA_HEADS)]
    tile = run_branch(kw_ref, wtab_ref, HEAD_DIM, q2T_win, 0, None)
    nprev = NSA_WINDOW // TL

    @pl.when(i >= nprev)
    def _():
        tile(i - nprev, kk > qk)

    for d in range(nprev - 1, 0, -1):
        @pl.when(i >= d)
        def _():
            tile(i - d, None)

    tile(i, kk <= qk)

    for h in range(NSA_HEADS):
        o_w = acc_ref[h] / l_ref[h, 0:1, :]
        g = gT_ref[0, 0]
        out = (g[h:h + 1, :] * o_c[h] + g[NSA_HEADS + h:NSA_HEADS + h + 1, :] * o_s[h]
               + g[2 * NSA_HEADS + h:2 * NSA_HEADS + h + 1, :] * o_w)
        o_ref[0, 0, h * HEAD_DIM:(h + 1) * HEAD_DIM, :] = out.astype(BF16)


def _nsa_tables(S, nc, nsel):
    pos = np.arange(S)
    onehot = jnp.asarray((pos[:, None] // NSA_SEL_LEN == np.arange(nsel)[None, :]).astype(np.float32))
    pfeat = jnp.stack(_split3(jnp.asarray(pos, F32)), axis=1).astype(F32)
    rest = jnp.zeros((S, LANES - nsel - 3), F32)
    stab = jnp.concatenate([onehot, pfeat, rest], axis=1).astype(BF16)
    wtab = jnp.concatenate([jnp.zeros_like(onehot), pfeat, rest], axis=1).astype(BF16)
    cend = np.arange(nc) * NSA_CMP_STRIDE + NSA_CMP_LEN - 1
    cfeat = jnp.stack(_split3(jnp.asarray(cend, F32)), axis=1).astype(F32)
    ctab = jnp.concatenate([jnp.zeros((nc, nsel), F32), cfeat, jnp.zeros((nc, LANES - nsel - 3), F32)], axis=1).astype(BF16)
    ratio = NSA_SEL_LEN // NSA_CMP_STRIDE
    span = NSA_CMP_LEN // NSA_CMP_STRIDE
    a = np.zeros((nsel, nc), np.float32)
    for j in range(nsel):
        for mm in range(ratio):
            for n2 in range(span):
                if j * ratio + mm + n2 < nc:
                    a[j, j * ratio + mm + n2] += 1.0
    return stab, wtab, ctab, jnp.asarray(a)


def _nsa_attention(p, kcmp, vcmpT):
    B, nT = p["nqT"].shape[:2]
    S = nT * TL
    nc = kcmp.shape[1]
    nsel = S // NSA_SEL_LEN
    stab, wtab, ctab, aT = _nsa_tables(S, nc, nsel)
    full = lambda a: pl.BlockSpec(a.shape, lambda b, i: (0,) * a.ndim)
    return pl.pallas_call(
        functools.partial(_nsa_body, nc=nc, nsel=nsel),
        grid=(B, nT),
        in_specs=[pl.BlockSpec((1, 1, NSA_W, TL), lambda b, i: (b, i, 0, 0)),
                  pl.BlockSpec((1, nc, LANES), lambda b, i: (b, 0, 0)),
                  pl.BlockSpec((1, HEAD_DIM, nc), lambda b, i: (b, 0, 0)),
                  full(ctab), full(aT),
                  pl.BlockSpec((1, S, LANES), lambda b, i: (b, 0, 0)),
                  pl.BlockSpec((1, S, LANES), lambda b, i: (b, 0, 0)),
                  pl.BlockSpec((1, nT, 2 * HEAD_DIM, TL), lambda b, i: (b, 0, 0, 0)),
                  pl.BlockSpec((1, 1, 16, TL), lambda b, i: (b, i, 0, 0)),
                  full(stab), full(wtab)],
        out_specs=pl.BlockSpec((1, 1, NSA_W, TL), lambda b, i: (b, i, 0, 0)),
        out_shape=jax.ShapeDtypeStruct((B, nT, NSA_W, TL), BF16),
        scratch_shapes=[pltpu.VMEM((nsel, TL), F32), pltpu.VMEM((NSA_HEADS, HEAD_DIM, TL), F32),
                        pltpu.VMEM((NSA_HEADS, 8, TL), F32), pltpu.VMEM((NSA_HEADS, 8, TL), F32)],
        compiler_params=pltpu.CompilerParams(dimension_semantics=("arbitrary",) * 2, vmem_limit_bytes=VMEM_LIMIT),
        name="nsa_attn",
    )(p["nqT"], kcmp, vcmpT, ctab, aT, p["ks"], p["kw"], p["nvT"], p["gT"], stab, wtab)


def _merge_body(x_ref, g_ref, ofT_ref, onT_ref, omT_ref, wfT_ref, wnT_ref, wmT_ref, wgT_ref, bg_ref, woT_ref, out_ref, outT_ref):
    x = x_ref[0]
    hb = _rms(x, g_ref[...]).astype(BF16)
    ys = (_dot(wfT_ref[...], ofT_ref[0, 0]), _dot(wnT_ref[...], onT_ref[0, 0]), _dot(wmT_ref[...], omT_ref[0, 0]))
    merged = jnp.zeros_like(ys[0])
    for r in range(N_BRANCHES):
        rows = slice(r * D_MODEL, (r + 1) * D_MODEL)
        gate = jax.nn.sigmoid(_nt_dot(wgT_ref[rows, :], hb) + bg_ref[rows, :])
        merged = merged + gate * ys[r]
    outT_ref[...] = _dot(woT_ref[...], merged.astype(BF16))
    out_ref[0] = x + outT_ref[...].T


def _merge(x, g, ofT, onT, omT, w_br_fox, w_br_nsa, w_br_moba, w_gate, b_gate, w_out):
    B, S, D = x.shape
    nT = S // TL
    wfT, wnT, wmT = w_br_fox.T.astype(BF16), w_br_nsa.T.astype(BF16), w_br_moba.T.astype(BF16)
    wgT, bg, woT = w_gate.T.astype(BF16), b_gate[:, None], w_out.T.astype(BF16)
    full = lambda a: pl.BlockSpec(a.shape, lambda b, i: (0,) * a.ndim)
    tiled = lambda n: pl.BlockSpec((1, 1, n, TL), lambda b, i: (b, i, 0, 0))
    return pl.pallas_call(
        _merge_body,
        grid=(B, nT),
        in_specs=[pl.BlockSpec((1, TL, D), lambda b, i: (b, i, 0)), full(g), tiled(FOX_W), tiled(NSA_W), tiled(MOBA_W),
                  full(wfT), full(wnT), full(wmT), full(wgT), full(bg), full(woT)],
        out_specs=pl.BlockSpec((1, TL, D), lambda b, i: (b, i, 0)),
        out_shape=jax.ShapeDtypeStruct((B, S, D), F32),
        scratch_shapes=[pltpu.VMEM((D, TL), F32)],
        compiler_params=pltpu.CompilerParams(dimension_semantics=("arbitrary",) * 2, vmem_limit_bytes=VMEM_LIMIT),
        name="merge",
    )(x, g, ofT, onT, omT, wfT, wnT, wmT, wgT, bg, woT)


def _route(lg):
    def softmax_rows(rows):
        m = functools.reduce(jnp.maximum, rows)
        es = [jnp.exp(r - m) for r in rows]
        tot = functools.reduce(lambda a, b: a + b, es)
        return [e / tot for e in es]

    def rank_of(vals, n):
        r = jnp.zeros_like(vals[0])
        for n2 in range(len(vals)):
            if n2 == n:
                continue
            ahead = (vals[n2] > vals[n]) | ((vals[n2] == vals[n]) & (n2 < n))
            r = r + jnp.where(ahead, 1.0, 0.0)
        return r

    gl = [lg[r:r + 1, :] for r in range(N_GROUPS)]
    gp = softmax_rows(gl)
    out = []
    for gi in range(N_GROUPS):
        gw = jnp.where(rank_of(gl, gi) < 1.0, gp[gi], 0.0)
        ep = softmax_rows([lg[8 + gi * EXPERTS_PER_GROUP + e: 9 + gi * EXPERTS_PER_GROUP + e, :]
                           for e in range(EXPERTS_PER_GROUP)])
        top = [jnp.where(rank_of(ep, e) < float(TOPK_IN_GROUP), ep[e], 0.0) for e in range(EXPERTS_PER_GROUP)]
        tot = functools.reduce(lambda a, b: a + b, top)
        out.extend([gw * (t / tot) for t in top])
    return jnp.concatenate(out, axis=0)


def _moe_body(x_ref, g_ref, wr_ref, br_ref, wgT_ref, wuT_ref, wdT_ref, gf_ref, out_ref, hb_ref, comb_ref, acc_ref, *, final):
    e = pl.program_id(1)

    @pl.when(e == 0)
    def _():
        h = _rms(x_ref[...], g_ref[...])
        hb_ref[...] = h.astype(BF16)
        comb_ref[...] = _route(_nt_dot(wr_ref[...], h, precision=HIGHEST) + br_ref[...])
        acc_ref[...] = jnp.zeros_like(acc_ref)

    hb = hb_ref[...]
    a = _nt_dot(wgT_ref[0], hb)
    u = _nt_dot(wuT_ref[0], hb)
    hid = (a * jax.nn.sigmoid(a)) * u * comb_ref[pl.ds(e, 1), :]
    acc_ref[...] += _dot(wdT_ref[0], hid.astype(BF16))

    @pl.when(e == N_EXPERTS - 1)
    def _():
        y = x_ref[...] + acc_ref[...].T
        out_ref[...] = _rms(y, gf_ref[...]) if final else y


def _moe(x, g, w_route_grp, b_route_grp, w_route_exp, b_route_exp, w_exp_gate, w_exp_up, w_exp_down, g_final, final, tm=512):
    B, S, D = x.shape
    T = B * S
    xt = x.reshape(T, D)
    wr = jnp.zeros((32, D), F32).at[0:N_GROUPS].set(w_route_grp.T).at[8:8 + N_EXPERTS].set(w_route_exp.T)
    br = jnp.zeros((32, 1), F32).at[0:N_GROUPS, 0].set(b_route_grp).at[8:8 + N_EXPERTS, 0].set(b_route_exp)
    wgT = jnp.swapaxes(w_exp_gate, 1, 2).astype(BF16)
    wuT = jnp.swapaxes(w_exp_up, 1, 2).astype(BF16)
    wdT = jnp.swapaxes(w_exp_down, 1, 2).astype(BF16)
    full = lambda a: pl.BlockSpec(a.shape, lambda t, e: (0,) * a.ndim)
    out = pl.pallas_call(
        functools.partial(_moe_body, final=final),
        grid=(T // tm, N_EXPERTS),
        in_specs=[pl.BlockSpec((tm, D), lambda t, e: (t, 0)), full(g), full(wr), full(br),
                  pl.BlockSpec((1, EXPERT_HIDDEN, D), lambda t, e: (e, 0, 0)),
                  pl.BlockSpec((1, EXPERT_HIDDEN, D), lambda t, e: (e, 0, 0)),
                  pl.BlockSpec((1, D, EXPERT_HIDDEN), lambda t, e: (e, 0, 0)),
                  full(g_final)],
        out_specs=pl.BlockSpec((tm, D), lambda t, e: (t, 0)),
        out_shape=jax.ShapeDtypeStruct((T, D), F32),
        scratch_shapes=[pltpu.VMEM((tm, D), BF16), pltpu.VMEM((N_EXPERTS, tm), F32), pltpu.VMEM((D, tm), F32)],
        compiler_params=pltpu.CompilerParams(dimension_semantics=("arbitrary",) * 2, vmem_limit_bytes=VMEM_LIMIT),
        name="moe",
    )(xt, g, wr, br, wgT, wuT, wdT, g_final)
    return out.reshape(B, S, D)


def _fox_placement():
    e = np.zeros((FOX_HEADS, 3, LANES, LANES), np.float32)
    for h in range(FOX_HEADS):
        for j in range(3):
            e[h, j, h, j] = 1.0
    return jnp.asarray(e, BF16)


def kernel(x, norm_mix_g, norm_ffn_g, final_norm_g, w_in, b_in, cmp_w1, cmp_b1, cmp_w2, cmp_pos, w_br_fox, w_br_nsa, w_br_moba, w_out, w_route_grp, b_route_grp, w_route_exp, b_route_exp, w_exp_gate, w_exp_up, w_exp_down):
    depth = w_in.shape[0]
    e_fox = _fox_placement()
    gf = final_norm_g[None, :]
    for l in range(depth):
        g_mix = norm_mix_g[l][None, :]
        ws, bs, wt, bt = _proj_weights(w_in[l], b_in[l])
        p = _project(x, g_mix, ws, bs, wt, bt)
        ofT = _fox_attention(p, e_fox)
        kcmp, vcmpT = _nsa_compress(p, cmp_w1[l], cmp_b1[l], cmp_w2[l], cmp_pos[l])
        onT = _nsa_attention(p, kcmp, vcmpT)
        omT = _moba_attention(p)
        x = _merge(x, g_mix, ofT, onT, omT, w_br_fox[l], w_br_nsa[l], w_br_moba[l],
                   w_in[l][:, _O_ML:_O_END], b_in[l][_O_ML:_O_END], w_out[l])
        x = _moe(x, norm_ffn_g[l][None, :], w_route_grp[l], b_route_grp[l], w_route_exp[l], b_route_exp[l],
                 w_exp_gate[l], w_exp_up[l], w_exp_down[l], gf, final=(l == depth - 1))
    return x
```

```python
import functools

import numpy as np
import jax
import jax.numpy as jnp
from jax import lax
from jax.experimental import pallas as pl
from jax.experimental.pallas import tpu as pltpu

F32 = jnp.float32
BF16 = jnp.bfloat16
HIGHEST = lax.Precision.HIGHEST

D_MODEL = 1024
HEAD_DIM = 64
FOX_HEADS = 6
NSA_HEADS = 4
MOBA_HEADS = 6
FOX_W = FOX_HEADS * HEAD_DIM
NSA_W = NSA_HEADS * HEAD_DIM
MOBA_W = MOBA_HEADS * HEAD_DIM
N_BRANCHES = 3

NSA_CMP_LEN = 32
NSA_CMP_STRIDE = 16
NSA_CMP_HIDDEN = 256
NSA_SEL_LEN = 64
NSA_TOPN = 16
NSA_LOCAL = 2
NSA_WINDOW = 512
MOBA_BLOCK = 256
MOBA_TOPK = 3

N_GROUPS = 4
EXPERTS_PER_GROUP = 4
N_EXPERTS = N_GROUPS * EXPERTS_PER_GROUP
TOPK_IN_GROUP = 2
EXPERT_HIDDEN = 256

RMS_EPS = 1e-6
NEG_INF = -1e30
FORCE_SCORE = 1e9

LANES = 128
TL = 256
KAUG = 2 * LANES
VMEM_LIMIT = 56 * 1024 * 1024

_O_FQ = 0
_O_FK = _O_FQ + FOX_W
_O_FV = _O_FK + FOX_W
_O_FF = _O_FV + FOX_W
_O_NQ = _O_FF + FOX_HEADS
_O_KC = _O_NQ + NSA_W
_O_VC = _O_KC + HEAD_DIM
_O_KS = _O_VC + HEAD_DIM
_O_VS = _O_KS + HEAD_DIM
_O_KW = _O_VS + HEAD_DIM
_O_VW = _O_KW + HEAD_DIM
_O_NG = _O_VW + HEAD_DIM
_O_MQ = _O_NG + N_BRANCHES * NSA_HEADS
_O_MK = _O_MQ + MOBA_W
_O_MV = _O_MK + MOBA_W
_O_ML = _O_MV + MOBA_W
_O_END = _O_ML + N_BRANCHES * D_MODEL

_S_FK = 0
_S_MK = _S_FK + FOX_HEADS * LANES
_S_KS = _S_MK + MOBA_HEADS * LANES
_S_KW = _S_KS + LANES
_S_KC = _S_KW + LANES
_S_VC = _S_KC + LANES
_S_FF = _S_VC + LANES
_S_END = _S_FF + LANES
_T_FQ = 0
_T_FV = _T_FQ + FOX_W
_T_MQ = _T_FV + FOX_W
_T_MV = _T_MQ + MOBA_W
_T_NQ = _T_MV + MOBA_W
_T_NV = _T_NQ + NSA_W
_T_NG = _T_NV + 2 * HEAD_DIM
_T_END = _T_NG + 16


def _alibi_slopes(n):
    return 2.0 ** (-8.0 * np.arange(1, n + 1) / n)


def _split3(v):
    hi = v.astype(BF16)
    r1 = v - hi.astype(F32)
    lo = r1.astype(BF16)
    r2 = r1 - lo.astype(F32)
    return hi, lo, r2.astype(BF16)


def _nt_dot(a, b, precision=None):
    return lax.dot_general(a, b, (((1,), (1,)), ((), ())), preferred_element_type=F32, precision=precision)


def _dot(a, b, precision=None):
    return jnp.dot(a, b, preferred_element_type=F32, precision=precision)


def _rms(x, g):
    return x * lax.rsqrt(jnp.mean(x * x, axis=-1, keepdims=True) + RMS_EPS) * g


def _proj_body(x_ref, g_ref, ws_ref, bs_ref, wt_ref, bt_ref,
               fk_ref, mk_ref, ks_ref, kw_ref, kc_ref, vc_ref, negc_ref, kmean_ref,
               fqT_ref, fvT_ref, mqT_ref, mvT_ref, nqT_ref, nvT_ref, gT_ref, carry_ref, *, tm):
    i = pl.program_id(1)
    hb = _rms(x_ref[0], g_ref[...]).astype(BF16)

    def sdot(a, b):
        return _dot(hb, ws_ref[:, a:b]) + bs_ref[:, a:b]

    def tdot(a, b):
        return _nt_dot(wt_ref[a:b, :], hb) + bt_ref[a:b, :]

    def put_t(ref, val):
        for c in range(tm // TL):
            ref[0, c] = val[:, c * TL:(c + 1) * TL]

    fk_ref[0] = sdot(_S_FK, _S_MK).astype(BF16)
    mk = sdot(_S_MK, _S_KS)
    mk_ref[0] = mk.astype(BF16)
    nblk = tm // MOBA_BLOCK
    means = [jnp.mean(mk[c * MOBA_BLOCK:(c + 1) * MOBA_BLOCK, :], axis=0, keepdims=True) for c in range(nblk)]
    kmean_ref[0, 0] = jnp.concatenate(means + [jnp.zeros((8 - nblk, mk.shape[1]), F32)], axis=0)
    ks_ref[0] = sdot(_S_KS, _S_KW).astype(BF16)
    kw_ref[0] = sdot(_S_KW, _S_KC).astype(BF16)
    kc_ref[0] = sdot(_S_KC, _S_VC)
    vc_ref[0] = sdot(_S_VC, _S_FF)

    @pl.when(i == 0)
    def _():
        carry_ref[...] = jnp.zeros_like(carry_ref)

    fl = sdot(_S_FF, _S_END)
    ls = jnp.minimum(fl, 0.0) - jnp.log1p(jnp.exp(-jnp.abs(fl)))
    ls = jnp.where(lax.broadcasted_iota(jnp.int32, ls.shape, 1) < FOX_HEADS, ls, 0.0)
    tri = (lax.broadcasted_iota(jnp.int32, (tm, tm), 1) <= lax.broadcasted_iota(jnp.int32, (tm, tm), 0)).astype(F32)
    cs = _dot(tri, ls, precision=HIGHEST) + carry_ref[0:1, :]
    negc_ref[0] = -cs
    carry_ref[0:1, :] = cs[tm - 1:tm, :]

    put_t(fqT_ref, tdot(_T_FQ, _T_FV).astype(BF16))
    put_t(fvT_ref, tdot(_T_FV, _T_MQ).astype(BF16))
    put_t(mqT_ref, tdot(_T_MQ, _T_MV).astype(BF16))
    put_t(mvT_ref, tdot(_T_MV, _T_NQ).astype(BF16))
    put_t(nqT_ref, tdot(_T_NQ, _T_NV).astype(BF16))
    put_t(nvT_ref, tdot(_T_NV, _T_NG).astype(BF16))
    put_t(gT_ref, jax.nn.sigmoid(tdot(_T_NG, _T_END)))


def _proj_weights(w, b):
    def pad_heads(cols, nh):
        m = cols.reshape(cols.shape[0], nh, HEAD_DIM)
        return jnp.pad(m, ((0, 0), (0, 0), (0, LANES - HEAD_DIM))).reshape(cols.shape[0], nh * LANES)

    def pad_to(cols, n):
        return jnp.pad(cols, ((0, 0), (0, n - cols.shape[1])))

    wb = jnp.concatenate([w, b[None, :]], axis=0)
    sl = lambda o, n: wb[:, o:o + n]
    std = jnp.concatenate([
        pad_heads(sl(_O_FK, FOX_W), FOX_HEADS), pad_heads(sl(_O_MK, MOBA_W), MOBA_HEADS),
        pad_to(sl(_O_KS, HEAD_DIM), LANES), pad_to(sl(_O_KW, HEAD_DIM), LANES),
        pad_to(sl(_O_KC, HEAD_DIM), LANES), pad_to(sl(_O_VC, HEAD_DIM), LANES),
        pad_to(sl(_O_FF, FOX_HEADS), LANES)], axis=1)
    scale = HEAD_DIM ** -0.5
    tr = jnp.concatenate([
        sl(_O_FQ, FOX_W) * scale, sl(_O_FV, FOX_W), sl(_O_MQ, MOBA_W) * scale, sl(_O_MV, MOBA_W),
        sl(_O_NQ, NSA_W) * scale, sl(_O_VS, HEAD_DIM), sl(_O_VW, HEAD_DIM),
        pad_to(sl(_O_NG, N_BRANCHES * NSA_HEADS), 16)], axis=1)
    ws, bs = std[:-1].astype(BF16), std[-1:].astype(F32)
    wt, bt = tr[:-1].T.astype(BF16), tr[-1:].T.astype(F32)
    return ws, bs, wt, bt


def _project(x, g, ws, bs, wt, bt, tm=512):
    B, S, D = x.shape
    nT = S // TL
    nt = tm // TL
    grid = (B, S // tm)
    row = lambda n, dt: jax.ShapeDtypeStruct((B, S, n), dt)
    tiled = lambda n, dt: jax.ShapeDtypeStruct((B, nT, n, TL), dt)
    row_spec = lambda n: pl.BlockSpec((1, tm, n), lambda b, i: (b, i, 0))
    tiled_spec = lambda n: pl.BlockSpec((1, nt, n, TL), lambda b, i: (b, i, 0, 0))
    full = lambda a: pl.BlockSpec(a.shape, lambda b, i: (0,) * a.ndim)
    out_shape = [row(FOX_HEADS * LANES, BF16), row(MOBA_HEADS * LANES, BF16), row(LANES, BF16), row(LANES, BF16),
                 row(LANES, F32), row(LANES, F32), row(LANES, F32),
                 jax.ShapeDtypeStruct((B, S // tm, 8, MOBA_HEADS * LANES), F32),
                 tiled(FOX_W, BF16), tiled(FOX_W, BF16), tiled(MOBA_W, BF16), tiled(MOBA_W, BF16),
                 tiled(NSA_W, BF16), tiled(2 * HEAD_DIM, BF16), tiled(16, F32)]
    out_specs = [row_spec(FOX_HEADS * LANES), row_spec(MOBA_HEADS * LANES), row_spec(LANES), row_spec(LANES),
                 row_spec(LANES), row_spec(LANES), row_spec(LANES),
                 pl.BlockSpec((1, 1, 8, MOBA_HEADS * LANES), lambda b, i: (b, i, 0, 0)),
                 tiled_spec(FOX_W), tiled_spec(FOX_W), tiled_spec(MOBA_W), tiled_spec(MOBA_W),
                 tiled_spec(NSA_W), tiled_spec(2 * HEAD_DIM), tiled_spec(16)]
    outs = pl.pallas_call(
        functools.partial(_proj_body, tm=tm),
        grid=grid,
        in_specs=[pl.BlockSpec((1, tm, D), lambda b, i: (b, i, 0)), full(g), full(ws), full(bs), full(wt), full(bt)],
        out_specs=out_specs,
        out_shape=out_shape,
        scratch_shapes=[pltpu.VMEM((8, LANES), F32)],
        compiler_params=pltpu.CompilerParams(dimension_semantics=("arbitrary", "arbitrary"),
                                             vmem_limit_bytes=VMEM_LIMIT),
        name="proj",
    )(x, g, ws, bs, wt, bt)
    (fk, mk, ks, kw, kc, vc, negc, kmean8, fqT, fvT, mqT, mvT, nqT, nvT, gT) = outs
    nblk = tm // MOBA_BLOCK
    kmean = kmean8[:, :, :nblk, :].reshape(B, (S // tm) * nblk, MOBA_HEADS * LANES)
    return dict(fk=fk, mk=mk, ks=ks, kw=kw, kc=kc, vc=vc, negc=negc, kmean=kmean,
                fqT=fqT, fvT=fvT, mqT=mqT, mvT=mvT, nqT=nqT, nvT=nvT, gT=gT)


def _flash_sweep(n_heads, diag_tile, n_rest, rest_tile, scores, value_tile, bufs, m_ref, l_ref, acc_ref):
    half = n_heads // 2
    groups = (tuple(range(half)), tuple(range(half, n_heads)))
    for h in range(n_heads):
        m_ref[h] = jnp.full(m_ref.shape[1:], NEG_INF, F32)
        l_ref[h] = jnp.zeros(l_ref.shape[1:], F32)
        acc_ref[h] = jnp.zeros(acc_ref.shape[1:], F32)

    def produce(j, grp, t):
        mts = []
        for h in grp:
            sT = scores(j, h, t)
            bufs[h][...] = sT
            mts.append(jnp.max(sT, axis=0, keepdims=True))
        return tuple(mts)

    def consume(j, grp, mts):
        for h, mt in zip(grp, mts):
            m = m_ref[h, 0:1, :]
            m_new = jnp.maximum(m, mt)
            p = jnp.exp(bufs[h][...] - m_new)
            alpha = jnp.exp(m - m_new)
            m_ref[h, 0:1, :] = m_new
            l_ref[h, 0:1, :] = alpha * l_ref[h, 0:1, :] + jnp.sum(p, axis=0, keepdims=True)
            acc_ref[h] = alpha * acc_ref[h] + _dot(value_tile(j, h), p.astype(BF16))

    def tile_steps(j, t, prev):
        mts0 = produce(j, groups[0], t)
        if prev is not None:
            consume(prev[0], groups[1], prev[1])
        mts1 = produce(j, groups[1], t)
        consume(j, groups[0], mts0)
        return mts1

    def body(t, mts):
        prev_tile = jnp.where(t == 0, diag_tile, rest_tile(t - 1))
        return tile_steps(rest_tile(t), t, (prev_tile, mts))

    mts = lax.fori_loop(0, n_rest, body, tile_steps(diag_tile, None, None))
    consume(jnp.where(n_rest == 0, diag_tile, rest_tile(n_rest - 1)), groups[1], mts)


def _tile_iotas(tk, tq):
    kk = lax.broadcasted_iota(jnp.int32, (tk, tq), 0)
    qq = lax.broadcasted_iota(jnp.int32, (tk, tq), 1)
    return kk, qq


def _feature_rows(nrows, tq, value):
    r = lax.broadcasted_iota(jnp.int32, (16, tq), 0)
    return jnp.where(r < nrows, value, 0.0).astype(F32)


def _causal_keep(tq):
    kk, qq = _tile_iotas(TL, tq)
    return kk <= qq


def _flash_scratch(n_heads):
    return ([pltpu.VMEM((TL, TL), F32) for _ in range(n_heads)]
            + [pltpu.VMEM((n_heads, 8, TL), F32), pltpu.VMEM((n_heads, 8, TL), F32),
               pltpu.VMEM((n_heads, HEAD_DIM, TL), F32)])


def _fox_body(qT_ref, k_ref, vT_ref, negc_ref, e_ref, o_ref, kfeat_ref, q2_ref, *flash, S, hps):
    bufs, (m_ref, l_ref, acc_ref) = flash[:hps], flash[hps:]
    i = pl.program_id(2)
    tq = TL

    @pl.when(i == 0)
    def _():
        chunk = 512
        for c in range(S // chunk):
            hi, lo, lo2 = _split3(negc_ref[0, c * chunk:(c + 1) * chunk, :])
            for h in range(hps):
                feat = _dot(hi, e_ref[h, 0]) + _dot(lo, e_ref[h, 1]) + _dot(lo2, e_ref[h, 2])
                kfeat_ref[h, c * chunk:(c + 1) * chunk, :] = feat.astype(BF16)

    feat_rows = jnp.concatenate([_feature_rows(3, tq, 1.0), jnp.zeros((LANES - 16, tq), F32)], axis=0).astype(BF16)
    for h in range(hps):
        q2_ref[h] = jnp.concatenate([qT_ref[0, 0, h * HEAD_DIM:(h + 1) * HEAD_DIM, :],
                                     jnp.zeros((HEAD_DIM, tq), BF16), feat_rows], axis=0)

    def scores(j, h, t):
        k2 = jnp.concatenate([k_ref[0, pl.ds(j * TL, TL), h * LANES:(h + 1) * LANES],
                              kfeat_ref[h, pl.ds(j * TL, TL), :]], axis=1)
        sT = _dot(k2, q2_ref[h])
        return jnp.where(_causal_keep(tq), sT, NEG_INF) if t is None else sT

    _flash_sweep(hps, i, i, lambda t: t, scores, lambda j, h: vT_ref[0, j, h * HEAD_DIM:(h + 1) * HEAD_DIM, :],
                 bufs, m_ref, l_ref, acc_ref)
    for h in range(hps):
        o_ref[0, 0, h * HEAD_DIM:(h + 1) * HEAD_DIM, :] = (acc_ref[h] / l_ref[h, 0:1, :]).astype(BF16)


def _fox_attention(p, e_fox, hps=FOX_HEADS):
    B, nT = p["fqT"].shape[:2]
    S = nT * TL
    return pl.pallas_call(
        functools.partial(_fox_body, S=S, hps=hps),
        grid=(B, FOX_HEADS // hps, nT),
        in_specs=[pl.BlockSpec((1, 1, hps * HEAD_DIM, TL), lambda b, h, i: (b, i, h, 0)),
                  pl.BlockSpec((1, S, hps * LANES), lambda b, h, i: (b, 0, h)),
                  pl.BlockSpec((1, nT, hps * HEAD_DIM, TL), lambda b, h, i: (b, 0, h, 0)),
                  pl.BlockSpec((1, S, LANES), lambda b, h, i: (b, 0, 0)),
                  pl.BlockSpec((hps, 3, LANES, LANES), lambda b, h, i: (h, 0, 0, 0))],
        out_specs=pl.BlockSpec((1, 1, hps * HEAD_DIM, TL), lambda b, h, i: (b, i, h, 0)),
        out_shape=jax.ShapeDtypeStruct((B, nT, FOX_W, TL), BF16),
        scratch_shapes=[pltpu.VMEM((hps, S, LANES), BF16), pltpu.VMEM((hps, KAUG, TL), BF16)] + _flash_scratch(hps),
        compiler_params=pltpu.CompilerParams(dimension_semantics=("arbitrary",) * 3, vmem_limit_bytes=VMEM_LIMIT),
        name="fox_attn",
    )(p["fqT"], p["fk"], p["fvT"], p["negc"], e_fox)


def _rank_rows(score, nrows):
    ridx = lax.broadcasted_iota(jnp.int32, score.shape, 0)
    rank = jnp.zeros(score.shape, F32)
    for n2 in range(nrows):
        row = score[n2:n2 + 1, :]
        ahead = (row > score) | ((row == score) & (n2 < ridx))
        rank = rank + jnp.where(ahead, 1.0, 0.0)
    return rank


def _moba_body(qT_ref, k_ref, vT_ref, kmean_ref, ptab_ref, o_ref, q2_ref, *flash, nbp, hps):
    bufs, (m_ref, l_ref, acc_ref) = flash[:hps], flash[hps:]
    i = pl.program_id(2)
    tq = TL
    zpad = jnp.zeros((HEAD_DIM, tq), BF16)
    blk = lax.broadcasted_iota(jnp.int32, (nbp, tq), 0)
    for h in range(hps):
        qT = qT_ref[0, 0, h * HEAD_DIM:(h + 1) * HEAD_DIM, :]
        gate = _dot(kmean_ref[0, :, h * LANES:(h + 1) * LANES].astype(BF16),
                    jnp.concatenate([qT, zpad], axis=0)) * (HEAD_DIM ** 0.5)
        gate = jnp.where(blk < i, gate, -FORCE_SCORE)
        sel = (_rank_rows(gate, nbp) < MOBA_TOPK) & (blk < i)
        selbias = jnp.where(sel | (blk == i), 0.0, NEG_INF)
        feat_rows = jnp.concatenate([selbias, _feature_rows(3, tq, 1.0), jnp.zeros((LANES - nbp - 16, tq), F32)], axis=0)
        q2_ref[h] = jnp.concatenate([qT, zpad, feat_rows.astype(BF16)], axis=0)

    def scores(j, h, t):
        k2 = jnp.concatenate([k_ref[0, pl.ds(j * TL, TL), h * LANES:(h + 1) * LANES],
                              ptab_ref[h, pl.ds(j * TL, TL), :]], axis=1)
        sT = _dot(k2, q2_ref[h])
        return jnp.where(_causal_keep(tq), sT, NEG_INF) if t is None else sT

    _flash_sweep(hps, i, i, lambda t: t, scores, lambda j, h: vT_ref[0, j, h * HEAD_DIM:(h + 1) * HEAD_DIM, :],
                 bufs, m_ref, l_ref, acc_ref)
    for h in range(hps):
        o_ref[0, 0, h * HEAD_DIM:(h + 1) * HEAD_DIM, :] = (acc_ref[h] / l_ref[h, 0:1, :]).astype(BF16)


def _moba_tables(S, nbp):
    pos = np.arange(S)
    onehot = (pos[:, None] // MOBA_BLOCK == np.arange(nbp)[None, :]).astype(np.float32)
    tabs = []
    for slope in _alibi_slopes(MOBA_HEADS):
        hi, lo, lo2 = _split3(jnp.asarray(slope * pos, F32))
        feat = jnp.stack([hi, lo, lo2], axis=1).astype(F32)
        tabs.append(jnp.concatenate([jnp.asarray(onehot), feat, jnp.zeros((S, LANES - nbp - 3), F32)], axis=1))
    return jnp.stack(tabs).astype(BF16)


def _moba_attention(p, hps=MOBA_HEADS):
    B, nT = p["mqT"].shape[:2]
    S = nT * TL
    nbp = -(-nT // 16) * 16
    kmean = jnp.pad(p["kmean"], ((0, 0), (0, nbp - nT), (0, 0)))
    ptab = _moba_tables(S, nbp)
    return pl.pallas_call(
        functools.partial(_moba_body, nbp=nbp, hps=hps),
        grid=(B, MOBA_HEADS // hps, nT),
        in_specs=[pl.BlockSpec((1, 1, hps * HEAD_DIM, TL), lambda b, h, i: (b, i, h, 0)),
                  pl.BlockSpec((1, S, hps * LANES), lambda b, h, i: (b, 0, h)),
                  pl.BlockSpec((1, nT, hps * HEAD_DIM, TL), lambda b, h, i: (b, 0, h, 0)),
                  pl.BlockSpec((1, nbp, hps * LANES), lambda b, h, i: (b, 0, h)),
                  pl.BlockSpec((hps, S, LANES), lambda b, h, i: (h, 0, 0))],
        out_specs=pl.BlockSpec((1, 1, hps * HEAD_DIM, TL), lambda b, h, i: (b, i, h, 0)),
        out_shape=jax.ShapeDtypeStruct((B, nT, MOBA_W, TL), BF16),
        scratch_shapes=[pltpu.VMEM((hps, KAUG, TL), BF16)] + _flash_scratch(hps),
        compiler_params=pltpu.CompilerParams(dimension_semantics=("arbitrary",) * 3, vmem_limit_bytes=VMEM_LIMIT),
        name="moba_attn",
    )(p["mqT"], p["mk"], p["mvT"], kmean, ptab)


def _cmp_body(kc_ref, vc_ref, w1_ref, b1_ref, w2k_ref, w2vT_ref, pos_ref, kcmp_ref, vcmpT_ref, pad_ref, *, S, nc):
    tail = pad_ref.shape[0] - S

    def hidden(src_ref, which):
        pad_ref[0:S, :] = src_ref[0]
        pad_ref[S:S + tail, :] = jnp.zeros((tail, LANES), F32)
        acc = jnp.zeros((nc, NSA_CMP_HIDDEN), F32)
        for l in range(NSA_CMP_LEN):
            rows = pad_ref[pl.ds(l, nc, stride=NSA_CMP_STRIDE), :] + pos_ref[which, l:l + 1, :]
            acc = acc + _dot(rows.astype(BF16), w1_ref[which, l])
        return jax.nn.gelu(acc + b1_ref[which]).astype(BF16)

    kcmp_ref[0] = _dot(hidden(kc_ref, 0), w2k_ref[...]).astype(BF16)
    vcmpT_ref[0] = _nt_dot(w2vT_ref[...], hidden(vc_ref, 1)).astype(BF16)


def _nsa_compress(p, cmp_w1, cmp_b1, cmp_w2, cmp_pos):
    B, S, _ = p["kc"].shape
    nc = S // NSA_CMP_STRIDE
    w1 = jnp.pad(cmp_w1, ((0, 0), (0, 0), (0, LANES - HEAD_DIM), (0, 0))).astype(BF16)
    pos = jnp.pad(cmp_pos, ((0, 0), (0, 0), (0, LANES - HEAD_DIM)))
    b1 = cmp_b1[:, None, :]
    w2k = jnp.pad(cmp_w2[0], ((0, 0), (0, LANES - HEAD_DIM))).astype(BF16)
    w2vT = cmp_w2[1].T.astype(BF16)
    full = lambda a: pl.BlockSpec(a.shape, lambda b: (0,) * a.ndim)
    return pl.pallas_call(
        functools.partial(_cmp_body, S=S, nc=nc),
        grid=(B,),
        in_specs=[pl.BlockSpec((1, S, LANES), lambda b: (b, 0, 0)), pl.BlockSpec((1, S, LANES), lambda b: (b, 0, 0)),
                  full(w1), full(b1), full(w2k), full(w2vT), full(pos)],
        out_specs=[pl.BlockSpec((1, nc, LANES), lambda b: (b, 0, 0)), pl.BlockSpec((1, HEAD_DIM, nc), lambda b: (b, 0, 0))],
        out_shape=[jax.ShapeDtypeStruct((B, nc, LANES), BF16), jax.ShapeDtypeStruct((B, HEAD_DIM, nc), BF16)],
        scratch_shapes=[pltpu.VMEM((S + NSA_CMP_LEN, LANES), F32)],
        compiler_params=pltpu.CompilerParams(dimension_semantics=("arbitrary",), vmem_limit_bytes=VMEM_LIMIT),
        name="nsa_compress",
    )(p["kc"], p["vc"], w1, b1, w2k, w2vT, pos)


def _nsa_body(qT_ref, kcmp_ref, vcmpT_ref, ctab_ref, aT_ref, ks_ref, kw_ref, vT_ref, gT_ref, stab_ref, wtab_ref,
              o_ref, score_ref, q2_ref, *flash, nc, nsel):
    bufs, (m_ref, l_ref, acc_ref) = flash[:NSA_HEADS], flash[NSA_HEADS:]
    i = pl.program_id(1)
    tq = TL
    t0 = i * tq
    slopes = _alibi_slopes(NSA_HEADS)
    zpad = jnp.zeros((HEAD_DIM, tq), BF16)

    def q2T_of(h, selbias):
        rows = jnp.concatenate([selbias, _feature_rows(3, tq, float(slopes[h])),
                                jnp.zeros((LANES - nsel - 16, tq), F32)], axis=0)
        return jnp.concatenate([qT_ref[0, 0, h * HEAD_DIM:(h + 1) * HEAD_DIM, :], zpad, rows.astype(BF16)], axis=0)

    kc2 = jnp.concatenate([kcmp_ref[0], ctab_ref[...]], axis=1)
    nn, qq = _tile_iotas(nc, tq)
    vis = nn * NSA_CMP_STRIDE + (NSA_CMP_LEN - 1) <= t0 + qq
    nosel = jnp.zeros((nsel, tq), F32)
    imp = jnp.zeros((nc, tq), F32)
    o_c = []
    for h in range(NSA_HEADS):
        sT = jnp.where(vis, _dot(kc2, q2T_of(h, nosel)), NEG_INF)
        m = jnp.max(sT, axis=0, keepdims=True)
        e = jnp.where(vis, jnp.exp(sT - m), 0.0)
        l = jnp.sum(e, axis=0, keepdims=True)
        pc = e / jnp.where(l > 0.0, l, 1.0)
        imp = imp + pc
        o_c.append(_dot(vcmpT_ref[0], pc.astype(BF16)))

    imp_sel = _dot(aT_ref[...], imp, precision=HIGHEST)
    jj, qs = _tile_iotas(nsel, tq)
    cur = (t0 + qs) // NSA_SEL_LEN
    causal = jj <= cur
    forced = (jj == 0) | (jj > cur - NSA_LOCAL)
    score = jnp.where(causal, jnp.where(forced, FORCE_SCORE, imp_sel), -FORCE_SCORE)
    score_ref[...] = score

    def rank_step(j2, rank):
        row = score_ref[pl.ds(j2, 1), :]
        ahead = (row > score) | ((row == score) & (j2 < jj))
        return rank + jnp.where(ahead, 1.0, 0.0)

    rank = lax.fori_loop(0, nsel, rank_step, jnp.zeros((nsel, tq), F32))
    selbias = jnp.where(rank < min(NSA_TOPN, nsel), 0.0, NEG_INF)

    kk, qk = _tile_iotas(TL, tq)

    def branch_scores(k_ref, tab_ref, oldest):
        def scores(j, h, t):
            k2 = jnp.concatenate([k_ref[0, pl.ds(j * TL, TL), :], tab_ref[pl.ds(j * TL, TL), :]], axis=1)
            sT = _dot(k2, q2_ref[h])
            if t is None:
                return jnp.where(kk <= qk, sT, NEG_INF)
            if oldest is not None:
                return jnp.where((t != oldest) | (kk > qk), sT, NEG_INF)
            return sT
        return scores

    for h in range(NSA_HEADS):
        q2_ref[h] = q2T_of(h, selbias)
    _flash_sweep(NSA_HEADS, i, i, lambda t: t, branch_scores(ks_ref, stab_ref, None),
                 lambda j, h: vT_ref[0, j, 0:HEAD_DIM, :], bufs, m_ref, l_ref, acc_ref)
    o_s = [acc_ref[h] / l_ref[h, 0:1, :] for h in range(NSA_HEADS)]

    for h in range(NSA_HEADS):
        q2_ref[h] = q2T_of(h, nosel)
    nprev = NSA_WINDOW // TL
    _flash_sweep(NSA_HEADS, i, jnp.minimum(i, nprev), lambda t: i - 1 - t,
                 branch_scores(kw_ref, wtab_ref, nprev - 1),
                 lambda j, h: vT_ref[0, j, HEAD_DIM:2 * HEAD_DIM, :], bufs, m_ref, l_ref, acc_ref)

    for h in range(NSA_HEADS):
        o_w = acc_ref[h] / l_ref[h, 0:1, :]
        g = gT_ref[0, 0]
        out = (g[h:h + 1, :] * o_c[h] + g[NSA_HEADS + h:NSA_HEADS + h + 1, :] * o_s[h]
               + g[2 * NSA_HEADS + h:2 * NSA_HEADS + h + 1, :] * o_w)
        o_ref[0, 0, h * HEAD_DIM:(h + 1) * HEAD_DIM, :] = out.astype(BF16)


def _nsa_tables(S, nc, nsel):
    pos = np.arange(S)
    onehot = jnp.asarray((pos[:, None] // NSA_SEL_LEN == np.arange(nsel)[None, :]).astype(np.float32))
    pfeat = jnp.stack(_split3(jnp.asarray(pos, F32)), axis=1).astype(F32)
    rest = jnp.zeros((S, LANES - nsel - 3), F32)
    stab = jnp.concatenate([onehot, pfeat, rest], axis=1).astype(BF16)
    wtab = jnp.concatenate([jnp.zeros_like(onehot), pfeat, rest], axis=1).astype(BF16)
    cend = np.arange(nc) * NSA_CMP_STRIDE + NSA_CMP_LEN - 1
    cfeat = jnp.stack(_split3(jnp.asarray(cend, F32)), axis=1).astype(F32)
    ctab = jnp.concatenate([jnp.zeros((nc, nsel), F32), cfeat, jnp.zeros((nc, LANES - nsel - 3), F32)], axis=1).astype(BF16)
    ratio = NSA_SEL_LEN // NSA_CMP_STRIDE
    span = NSA_CMP_LEN // NSA_CMP_STRIDE
    a = np.zeros((nsel, nc), np.float32)
    for j in range(nsel):
        for mm in range(ratio):
            for n2 in range(span):
                if j * ratio + mm + n2 < nc:
                    a[j, j * ratio + mm + n2] += 1.0
    return stab, wtab, ctab, jnp.asarray(a)


def _nsa_attention(p, kcmp, vcmpT):
    B, nT = p["nqT"].shape[:2]
    S = nT * TL
    nc = kcmp.shape[1]
    nsel = S // NSA_SEL_LEN
    stab, wtab, ctab, aT = _nsa_tables(S, nc, nsel)
    full = lambda a: pl.BlockSpec(a.shape, lambda b, i: (0,) * a.ndim)
    return pl.pallas_call(
        functools.partial(_nsa_body, nc=nc, nsel=nsel),
        grid=(B, nT),
        in_specs=[pl.BlockSpec((1, 1, NSA_W, TL), lambda b, i: (b, i, 0, 0)),
                  pl.BlockSpec((1, nc, LANES), lambda b, i: (b, 0, 0)),
                  pl.BlockSpec((1, HEAD_DIM, nc), lambda b, i: (b, 0, 0)),
                  full(ctab), full(aT),
                  pl.BlockSpec((1, S, LANES), lambda b, i: (b, 0, 0)),
                  pl.BlockSpec((1, S, LANES), lambda b, i: (b, 0, 0)),
                  pl.BlockSpec((1, nT, 2 * HEAD_DIM, TL), lambda b, i: (b, 0, 0, 0)),
                  pl.BlockSpec((1, 1, 16, TL), lambda b, i: (b, i, 0, 0)),
                  full(stab), full(wtab)],
        out_specs=pl.BlockSpec((1, 1, NSA_W, TL), lambda b, i: (b, i, 0, 0)),
        out_shape=jax.ShapeDtypeStruct((B, nT, NSA_W, TL), BF16),
        scratch_shapes=[pltpu.VMEM((nsel, TL), F32), pltpu.VMEM((NSA_HEADS, KAUG, TL), BF16)] + _flash_scratch(NSA_HEADS),
        compiler_params=pltpu.CompilerParams(dimension_semantics=("arbitrary",) * 2, vmem_limit_bytes=VMEM_LIMIT),
        name="nsa_attn",
    )(p["nqT"], kcmp, vcmpT, ctab, aT, p["ks"], p["kw"], p["nvT"], p["gT"], stab, wtab)


def _merge_body(x_ref, g_ref, ofT_ref, onT_ref, omT_ref, wfT_ref, wnT_ref, wmT_ref, wgT_ref, bg_ref, woT_ref, out_ref, outT_ref):
    x = x_ref[0]
    hb = _rms(x, g_ref[...]).astype(BF16)
    ys = (_dot(wfT_ref[...], ofT_ref[0, 0]), _dot(wnT_ref[...], onT_ref[0, 0]), _dot(wmT_ref[...], omT_ref[0, 0]))
    merged = jnp.zeros_like(ys[0])
    for r in range(N_BRANCHES):
        rows = slice(r * D_MODEL, (r + 1) * D_MODEL)
        gate = jax.nn.sigmoid(_nt_dot(wgT_ref[rows, :], hb) + bg_ref[rows, :])
        merged = merged + gate * ys[r]
    outT_ref[...] = _dot(woT_ref[...], merged.astype(BF16))
    out_ref[0] = x + outT_ref[...].T


def _merge(x, g, ofT, onT, omT, w_br_fox, w_br_nsa, w_br_moba, w_gate, b_gate, w_out):
    B, S, D = x.shape
    nT = S // TL
    wfT, wnT, wmT = w_br_fox.T.astype(BF16), w_br_nsa.T.astype(BF16), w_br_moba.T.astype(BF16)
    wgT, bg, woT = w_gate.T.astype(BF16), b_gate[:, None], w_out.T.astype(BF16)
    full = lambda a: pl.BlockSpec(a.shape, lambda b, i: (0,) * a.ndim)
    tiled = lambda n: pl.BlockSpec((1, 1, n, TL), lambda b, i: (b, i, 0, 0))
    return pl.pallas_call(
        _merge_body,
        grid=(B, nT),
        in_specs=[pl.BlockSpec((1, TL, D), lambda b, i: (b, i, 0)), full(g), tiled(FOX_W), tiled(NSA_W), tiled(MOBA_W),
                  full(wfT), full(wnT), full(wmT), full(wgT), full(bg), full(woT)],
        out_specs=pl.BlockSpec((1, TL, D), lambda b, i: (b, i, 0)),
        out_shape=jax.ShapeDtypeStruct((B, S, D), F32),
        scratch_shapes=[pltpu.VMEM((D, TL), F32)],
        compiler_params=pltpu.CompilerParams(dimension_semantics=("arbitrary",) * 2, vmem_limit_bytes=VMEM_LIMIT),
        name="merge",
    )(x, g, ofT, onT, omT, wfT, wnT, wmT, wgT, bg, woT)


def _route(lg):
    def softmax_rows(rows):
        m = functools.reduce(jnp.maximum, rows)
        es = [jnp.exp(r - m) for r in rows]
        tot = functools.reduce(lambda a, b: a + b, es)
        return [e / tot for e in es]

    def rank_of(vals, n):
        r = jnp.zeros_like(vals[0])
        for n2 in range(len(vals)):
            if n2 == n:
                continue
            ahead = (vals[n2] > vals[n]) | ((vals[n2] == vals[n]) & (n2 < n))
            r = r + jnp.where(ahead, 1.0, 0.0)
        return r

    gl = [lg[r:r + 1, :] for r in range(N_GROUPS)]
    gp = softmax_rows(gl)
    out = []
    for gi in range(N_GROUPS):
        gw = jnp.where(rank_of(gl, gi) < 1.0, gp[gi], 0.0)
        ep = softmax_rows([lg[8 + gi * EXPERTS_PER_GROUP + e: 9 + gi * EXPERTS_PER_GROUP + e, :]
                           for e in range(EXPERTS_PER_GROUP)])
        top = [jnp.where(rank_of(ep, e) < float(TOPK_IN_GROUP), ep[e], 0.0) for e in range(EXPERTS_PER_GROUP)]
        tot = functools.reduce(lambda a, b: a + b, top)
        out.extend([gw * (t / tot) for t in top])
    return jnp.concatenate(out, axis=0)


def _moe_body(x_ref, g_ref, wr_ref, br_ref, wgT_ref, wuT_ref, wdT_ref, gf_ref, out_ref, hb_ref, comb_ref, acc_ref, *, final):
    e = pl.program_id(1)

    @pl.when(e == 0)
    def _():
        h = _rms(x_ref[...], g_ref[...])
        hb_ref[...] = h.astype(BF16)
        comb_ref[...] = _route(_nt_dot(wr_ref[...], h, precision=HIGHEST) + br_ref[...])
        acc_ref[...] = jnp.zeros_like(acc_ref)

    hb = hb_ref[...]
    a = _nt_dot(wgT_ref[0], hb)
    u = _nt_dot(wuT_ref[0], hb)
    hid = (a * jax.nn.sigmoid(a)) * u * comb_ref[pl.ds(e, 1), :]
    acc_ref[...] += _dot(wdT_ref[0], hid.astype(BF16))

    @pl.when(e == N_EXPERTS - 1)
    def _():
        y = x_ref[...] + acc_ref[...].T
        out_ref[...] = _rms(y, gf_ref[...]) if final else y


def _moe(x, g, w_route_grp, b_route_grp, w_route_exp, b_route_exp, w_exp_gate, w_exp_up, w_exp_down, g_final, final, tm=512):
    B, S, D = x.shape
    T = B * S
    xt = x.reshape(T, D)
    wr = jnp.zeros((32, D), F32).at[0:N_GROUPS].set(w_route_grp.T).at[8:8 + N_EXPERTS].set(w_route_exp.T)
    br = jnp.zeros((32, 1), F32).at[0:N_GROUPS, 0].set(b_route_grp).at[8:8 + N_EXPERTS, 0].set(b_route_exp)
    wgT = jnp.swapaxes(w_exp_gate, 1, 2).astype(BF16)
    wuT = jnp.swapaxes(w_exp_up, 1, 2).astype(BF16)
    wdT = jnp.swapaxes(w_exp_down, 1, 2).astype(BF16)
    full = lambda a: pl.BlockSpec(a.shape, lambda t, e: (0,) * a.ndim)
    out = pl.pallas_call(
        functools.partial(_moe_body, final=final),
        grid=(T // tm, N_EXPERTS),
        in_specs=[pl.BlockSpec((tm, D), lambda t, e: (t, 0)), full(g), full(wr), full(br),
                  pl.BlockSpec((1, EXPERT_HIDDEN, D), lambda t, e: (e, 0, 0)),
                  pl.BlockSpec((1, EXPERT_HIDDEN, D), lambda t, e: (e, 0, 0)),
                  pl.BlockSpec((1, D, EXPERT_HIDDEN), lambda t, e: (e, 0, 0)),
                  full(g_final)],
        out_specs=pl.BlockSpec((tm, D), lambda t, e: (t, 0)),
        out_shape=jax.ShapeDtypeStruct((T, D), F32),
        scratch_shapes=[pltpu.VMEM((tm, D), BF16), pltpu.VMEM((N_EXPERTS, tm), F32), pltpu.VMEM((D, tm), F32)],
        compiler_params=pltpu.CompilerParams(dimension_semantics=("arbitrary",) * 2, vmem_limit_bytes=VMEM_LIMIT),
        name="moe",
    )(xt, g, wr, br, wgT, wuT, wdT, g_final)
    return out.reshape(B, S, D)


def _fox_placement():
    e = np.zeros((FOX_HEADS, 3, LANES, LANES), np.float32)
    for h in range(FOX_HEADS):
        for j in range(3):
            e[h, j, h, j] = 1.0
    return jnp.asarray(e, BF16)


def kernel(x, norm_mix_g, norm_ffn_g, final_norm_g, w_in, b_in, cmp_w1, cmp_b1, cmp_w2, cmp_pos, w_br_fox, w_br_nsa, w_br_moba, w_out, w_route_grp, b_route_grp, w_route_exp, b_route_exp, w_exp_gate, w_exp_up, w_exp_down):
    depth = w_in.shape[0]
    e_fox = _fox_placement()
    gf = final_norm_g[None, :]
    for l in range(depth):
        g_mix = norm_mix_g[l][None, :]
        ws, bs, wt, bt = _proj_weights(w_in[l], b_in[l])
        p = _project(x, g_mix, ws, bs, wt, bt)
        ofT = _fox_attention(p, e_fox)
        kcmp, vcmpT = _nsa_compress(p, cmp_w1[l], cmp_b1[l], cmp_w2[l], cmp_pos[l])
        onT = _nsa_attention(p, kcmp, vcmpT)
        omT = _moba_attention(p)
        x = _merge(x, g_mix, ofT, onT, omT, w_br_fox[l], w_br_nsa[l], w_br_moba[l],
                   w_in[l][:, _O_ML:_O_END], b_in[l][_O_ML:_O_END], w_out[l])
        x = _moe(x, norm_ffn_g[l][None, :], w_route_grp[l], b_route_grp[l], w_route_exp[l], b_route_exp[l],
                 w_exp_gate[l], w_exp_up[l], w_exp_down[l], gf, final=(l == depth - 1))
    return x
```

```python
import functools

import numpy as np
import jax
import jax.numpy as jnp
from jax import lax
from jax.experimental import pallas as pl
from jax.experimental.pallas import tpu as pltpu

F32 = jnp.float32
BF16 = jnp.bfloat16
HIGHEST = lax.Precision.HIGHEST

D_MODEL = 1024
HEAD_DIM = 64
FOX_HEADS = 6
NSA_HEADS = 4
MOBA_HEADS = 6
FOX_W = FOX_HEADS * HEAD_DIM
NSA_W = NSA_HEADS * HEAD_DIM
MOBA_W = MOBA_HEADS * HEAD_DIM
N_BRANCHES = 3

NSA_CMP_LEN = 32
NSA_CMP_STRIDE = 16
NSA_CMP_HIDDEN = 256
NSA_SEL_LEN = 64
NSA_TOPN = 16
NSA_LOCAL = 2
NSA_WINDOW = 512
MOBA_BLOCK = 256
MOBA_TOPK = 3

N_GROUPS = 4
EXPERTS_PER_GROUP = 4
N_EXPERTS = N_GROUPS * EXPERTS_PER_GROUP
TOPK_IN_GROUP = 2
EXPERT_HIDDEN = 256

RMS_EPS = 1e-6
NEG_INF = -1e30
FORCE_SCORE = 1e9

LANES = 128
TL = 256
KAUG = 2 * LANES
VMEM_LIMIT = 56 * 1024 * 1024

_O_FQ = 0
_O_FK = _O_FQ + FOX_W
_O_FV = _O_FK + FOX_W
_O_FF = _O_FV + FOX_W
_O_NQ = _O_FF + FOX_HEADS
_O_KC = _O_NQ + NSA_W
_O_VC = _O_KC + HEAD_DIM
_O_KS = _O_VC + HEAD_DIM
_O_VS = _O_KS + HEAD_DIM
_O_KW = _O_VS + HEAD_DIM
_O_VW = _O_KW + HEAD_DIM
_O_NG = _O_VW + HEAD_DIM
_O_MQ = _O_NG + N_BRANCHES * NSA_HEADS
_O_MK = _O_MQ + MOBA_W
_O_MV = _O_MK + MOBA_W
_O_ML = _O_MV + MOBA_W
_O_END = _O_ML + N_BRANCHES * D_MODEL

_S_FK = 0
_S_MK = _S_FK + FOX_HEADS * LANES
_S_KS = _S_MK + MOBA_HEADS * LANES
_S_KW = _S_KS + LANES
_S_KC = _S_KW + LANES
_S_VC = _S_KC + LANES
_S_FF = _S_VC + LANES
_S_END = _S_FF + LANES
_T_FQ = 0
_T_FV = _T_FQ + FOX_W
_T_MQ = _T_FV + FOX_W
_T_MV = _T_MQ + MOBA_W
_T_NQ = _T_MV + MOBA_W
_T_NV = _T_NQ + NSA_W
_T_NG = _T_NV + 2 * HEAD_DIM
_T_END = _T_NG + 16


def _alibi_slopes(n):
    return 2.0 ** (-8.0 * np.arange(1, n + 1) / n)


def _split3(v):
    hi = v.astype(BF16)
    r1 = v - hi.astype(F32)
    lo = r1.astype(BF16)
    r2 = r1 - lo.astype(F32)
    return hi, lo, r2.astype(BF16)


def _nt_dot(a, b, precision=None):
    return lax.dot_general(a, b, (((1,), (1,)), ((), ())), preferred_element_type=F32, precision=precision)


def _dot(a, b, precision=None):
    return jnp.dot(a, b, preferred_element_type=F32, precision=precision)


def _rms(x, g):
    return x * lax.rsqrt(jnp.mean(x * x, axis=-1, keepdims=True) + RMS_EPS) * g


def _proj_body(x_ref, g_ref, ws_ref, bs_ref, wt_ref, bt_ref,
               fk_ref, mk_ref, ks_ref, kw_ref, kc_ref, vc_ref, negc_ref, kmean_ref,
               fqT_ref, fvT_ref, mqT_ref, mvT_ref, nqT_ref, nvT_ref, gT_ref, carry_ref, *, tm):
    i = pl.program_id(1)
    hb = _rms(x_ref[0], g_ref[...]).astype(BF16)

    def sdot(a, b):
        return _dot(hb, ws_ref[:, a:b]) + bs_ref[:, a:b]

    def tdot(a, b):
        return _nt_dot(wt_ref[a:b, :], hb) + bt_ref[a:b, :]

    def put_t(ref, val):
        for c in range(tm // TL):
            ref[0, c] = val[:, c * TL:(c + 1) * TL]

    fk_ref[0] = sdot(_S_FK, _S_MK).astype(BF16)
    mk = sdot(_S_MK, _S_KS)
    mk_ref[0] = mk.astype(BF16)
    nblk = tm // MOBA_BLOCK
    means = [jnp.mean(mk[c * MOBA_BLOCK:(c + 1) * MOBA_BLOCK, :], axis=0, keepdims=True) for c in range(nblk)]
    kmean_ref[0, 0] = jnp.concatenate(means + [jnp.zeros((8 - nblk, mk.shape[1]), F32)], axis=0)
    ks_ref[0] = sdot(_S_KS, _S_KW).astype(BF16)
    kw_ref[0] = sdot(_S_KW, _S_KC).astype(BF16)
    kc_ref[0] = sdot(_S_KC, _S_VC)
    vc_ref[0] = sdot(_S_VC, _S_FF)

    @pl.when(i == 0)
    def _():
        carry_ref[...] = jnp.zeros_like(carry_ref)

    fl = sdot(_S_FF, _S_END)
    ls = jnp.minimum(fl, 0.0) - jnp.log1p(jnp.exp(-jnp.abs(fl)))
    ls = jnp.where(lax.broadcasted_iota(jnp.int32, ls.shape, 1) < FOX_HEADS, ls, 0.0)
    tri = (lax.broadcasted_iota(jnp.int32, (tm, tm), 1) <= lax.broadcasted_iota(jnp.int32, (tm, tm), 0)).astype(F32)
    cs = _dot(tri, ls, precision=HIGHEST) + carry_ref[0:1, :]
    negc_ref[0] = -cs
    carry_ref[0:1, :] = cs[tm - 1:tm, :]

    put_t(fqT_ref, tdot(_T_FQ, _T_FV).astype(BF16))
    put_t(fvT_ref, tdot(_T_FV, _T_MQ).astype(BF16))
    put_t(mqT_ref, tdot(_T_MQ, _T_MV).astype(BF16))
    put_t(mvT_ref, tdot(_T_MV, _T_NQ).astype(BF16))
    put_t(nqT_ref, tdot(_T_NQ, _T_NV).astype(BF16))
    put_t(nvT_ref, tdot(_T_NV, _T_NG).astype(BF16))
    put_t(gT_ref, jax.nn.sigmoid(tdot(_T_NG, _T_END)))


def _proj_weights(w, b):
    def pad_heads(cols, nh):
        m = cols.reshape(cols.shape[0], nh, HEAD_DIM)
        return jnp.pad(m, ((0, 0), (0, 0), (0, LANES - HEAD_DIM))).reshape(cols.shape[0], nh * LANES)

    def pad_to(cols, n):
        return jnp.pad(cols, ((0, 0), (0, n - cols.shape[1])))

    wb = jnp.concatenate([w, b[None, :]], axis=0)
    sl = lambda o, n: wb[:, o:o + n]
    std = jnp.concatenate([
        pad_heads(sl(_O_FK, FOX_W), FOX_HEADS), pad_heads(sl(_O_MK, MOBA_W), MOBA_HEADS),
        pad_to(sl(_O_KS, HEAD_DIM), LANES), pad_to(sl(_O_KW, HEAD_DIM), LANES),
        pad_to(sl(_O_KC, HEAD_DIM), LANES), pad_to(sl(_O_VC, HEAD_DIM), LANES),
        pad_to(sl(_O_FF, FOX_HEADS), LANES)], axis=1)
    scale = HEAD_DIM ** -0.5
    tr = jnp.concatenate([
        sl(_O_FQ, FOX_W) * scale, sl(_O_FV, FOX_W), sl(_O_MQ, MOBA_W) * scale, sl(_O_MV, MOBA_W),
        sl(_O_NQ, NSA_W) * scale, sl(_O_VS, HEAD_DIM), sl(_O_VW, HEAD_DIM),
        pad_to(sl(_O_NG, N_BRANCHES * NSA_HEADS), 16)], axis=1)
    ws, bs = std[:-1].astype(BF16), std[-1:].astype(F32)
    wt, bt = tr[:-1].T.astype(BF16), tr[-1:].T.astype(F32)
    return ws, bs, wt, bt


def _project(x, g, ws, bs, wt, bt, tm=512):
    B, S, D = x.shape
    nT = S // TL
    nt = tm // TL
    grid = (B, S // tm)
    row = lambda n, dt: jax.ShapeDtypeStruct((B, S, n), dt)
    tiled = lambda n, dt: jax.ShapeDtypeStruct((B, nT, n, TL), dt)
    row_spec = lambda n: pl.BlockSpec((1, tm, n), lambda b, i: (b, i, 0))
    tiled_spec = lambda n: pl.BlockSpec((1, nt, n, TL), lambda b, i: (b, i, 0, 0))
    full = lambda a: pl.BlockSpec(a.shape, lambda b, i: (0,) * a.ndim)
    out_shape = [row(FOX_HEADS * LANES, BF16), row(MOBA_HEADS * LANES, BF16), row(LANES, BF16), row(LANES, BF16),
                 row(LANES, F32), row(LANES, F32), row(LANES, F32),
                 jax.ShapeDtypeStruct((B, S // tm, 8, MOBA_HEADS * LANES), F32),
                 tiled(FOX_W, BF16), tiled(FOX_W, BF16), tiled(MOBA_W, BF16), tiled(MOBA_W, BF16),
                 tiled(NSA_W, BF16), tiled(2 * HEAD_DIM, BF16), tiled(16, F32)]
    out_specs = [row_spec(FOX_HEADS * LANES), row_spec(MOBA_HEADS * LANES), row_spec(LANES), row_spec(LANES),
                 row_spec(LANES), row_spec(LANES), row_spec(LANES),
                 pl.BlockSpec((1, 1, 8, MOBA_HEADS * LANES), lambda b, i: (b, i, 0, 0)),
                 tiled_spec(FOX_W), tiled_spec(FOX_W), tiled_spec(MOBA_W), tiled_spec(MOBA_W),
                 tiled_spec(NSA_W), tiled_spec(2 * HEAD_DIM), tiled_spec(16)]
    outs = pl.pallas_call(
        functools.partial(_proj_body, tm=tm),
        grid=grid,
        in_specs=[pl.BlockSpec((1, tm, D), lambda b, i: (b, i, 0)), full(g), full(ws), full(bs), full(wt), full(bt)],
        out_specs=out_specs,
        out_shape=out_shape,
        scratch_shapes=[pltpu.VMEM((8, LANES), F32)],
        compiler_params=pltpu.CompilerParams(dimension_semantics=("arbitrary", "arbitrary"),
                                             vmem_limit_bytes=VMEM_LIMIT),
        name="proj",
    )(x, g, ws, bs, wt, bt)
    (fk, mk, ks, kw, kc, vc, negc, kmean8, fqT, fvT, mqT, mvT, nqT, nvT, gT) = outs
    nblk = tm // MOBA_BLOCK
    kmean = kmean8[:, :, :nblk, :].reshape(B, (S // tm) * nblk, MOBA_HEADS * LANES)
    return dict(fk=fk, mk=mk, ks=ks, kw=kw, kc=kc, vc=vc, negc=negc, kmean=kmean,
                fqT=fqT, fvT=fvT, mqT=mqT, mvT=mvT, nqT=nqT, nvT=nvT, gT=gT)


def _flash_sweep(n_heads, diag_tile, n_rest, rest_tile, scores, value_tile, bufs, m_ref, l_ref, acc_ref):
    half = n_heads // 2
    groups = (tuple(range(half)), tuple(range(half, n_heads)))
    for h in range(n_heads):
        m_ref[h] = jnp.full(m_ref.shape[1:], NEG_INF, F32)
        l_ref[h] = jnp.zeros(l_ref.shape[1:], F32)
        acc_ref[h] = jnp.zeros(acc_ref.shape[1:], F32)

    def produce(j, grp, t):
        mts = []
        for h in grp:
            sT = scores(j, h, t)
            bufs[h][...] = sT
            mts.append(jnp.max(sT, axis=0, keepdims=True))
        return tuple(mts)

    def consume(j, grp, mts):
        for h, mt in zip(grp, mts):
            m = m_ref[h, 0:1, :]
            m_new = jnp.maximum(m, mt)
            p = jnp.exp(bufs[h][...] - m_new)
            alpha = jnp.exp(m - m_new)
            m_ref[h, 0:1, :] = m_new
            l_ref[h, 0:1, :] = alpha * l_ref[h, 0:1, :] + jnp.sum(p, axis=0, keepdims=True)
            acc_ref[h] = alpha * acc_ref[h] + _dot(value_tile(j, h), p.astype(BF16))

    def tile_steps(j, t, prev):
        mts0 = produce(j, groups[0], t)
        if prev is not None:
            consume(prev[0], groups[1], prev[1])
        mts1 = produce(j, groups[1], t)
        consume(j, groups[0], mts0)
        return mts1

    def body(t, mts):
        prev_tile = jnp.where(t == 0, diag_tile, rest_tile(t - 1))
        return tile_steps(rest_tile(t), t, (prev_tile, mts))

    mts = lax.fori_loop(0, n_rest, body, tile_steps(diag_tile, None, None))
    consume(jnp.where(n_rest == 0, diag_tile, rest_tile(n_rest - 1)), groups[1], mts)


def _tile_iotas(tk, tq):
    kk = lax.broadcasted_iota(jnp.int32, (tk, tq), 0)
    qq = lax.broadcasted_iota(jnp.int32, (tk, tq), 1)
    return kk, qq


def _feature_rows(nrows, tq, value):
    r = lax.broadcasted_iota(jnp.int32, (16, tq), 0)
    return jnp.where(r < nrows, value, 0.0).astype(F32)


def _causal_keep(tq):
    kk, qq = _tile_iotas(TL, tq)
    return kk <= qq


def _flash_scratch(n_heads):
    return ([pltpu.VMEM((TL, TL), F32) for _ in range(n_heads)]
            + [pltpu.VMEM((n_heads, 8, TL), F32), pltpu.VMEM((n_heads, 8, TL), F32),
               pltpu.VMEM((n_heads, HEAD_DIM, TL), F32)])


def _fox_body(qT_ref, k_ref, vT_ref, negc_ref, e_ref, o_ref, kfeat_ref, q2_ref, *flash, S, hps):
    bufs, (m_ref, l_ref, acc_ref) = flash[:hps], flash[hps:]
    i = pl.program_id(2)
    tq = TL

    @pl.when(i == 0)
    def _():
        chunk = 512
        for c in range(S // chunk):
            hi, lo, lo2 = _split3(negc_ref[0, c * chunk:(c + 1) * chunk, :])
            for h in range(hps):
                feat = _dot(hi, e_ref[h, 0]) + _dot(lo, e_ref[h, 1]) + _dot(lo2, e_ref[h, 2])
                kfeat_ref[h, c * chunk:(c + 1) * chunk, :] = feat.astype(BF16)

    feat_rows = jnp.concatenate([_feature_rows(3, tq, 1.0), jnp.zeros((LANES - 16, tq), F32)], axis=0).astype(BF16)
    for h in range(hps):
        q2_ref[h] = jnp.concatenate([qT_ref[0, 0, h * HEAD_DIM:(h + 1) * HEAD_DIM, :],
                                     jnp.zeros((HEAD_DIM, tq), BF16), feat_rows], axis=0)

    def scores(j, h, t):
        k2 = jnp.concatenate([k_ref[0, pl.ds(j * TL, TL), h * LANES:(h + 1) * LANES],
                              kfeat_ref[h, pl.ds(j * TL, TL), :]], axis=1)
        sT = _dot(k2, q2_ref[h])
        return jnp.where(_causal_keep(tq), sT, NEG_INF) if t is None else sT

    _flash_sweep(hps, i, i, lambda t: t, scores, lambda j, h: vT_ref[0, j, h * HEAD_DIM:(h + 1) * HEAD_DIM, :],
                 bufs, m_ref, l_ref, acc_ref)
    for h in range(hps):
        o_ref[0, 0, h * HEAD_DIM:(h + 1) * HEAD_DIM, :] = (acc_ref[h] / l_ref[h, 0:1, :]).astype(BF16)


def _fox_attention(p, e_fox, hps=FOX_HEADS):
    B, nT = p["fqT"].shape[:2]
    S = nT * TL
    return pl.pallas_call(
        functools.partial(_fox_body, S=S, hps=hps),
        grid=(B, FOX_HEADS // hps, nT),
        in_specs=[pl.BlockSpec((1, 1, hps * HEAD_DIM, TL), lambda b, h, i: (b, i, h, 0)),
                  pl.BlockSpec((1, S, hps * LANES), lambda b, h, i: (b, 0, h)),
                  pl.BlockSpec((1, nT, hps * HEAD_DIM, TL), lambda b, h, i: (b, 0, h, 0)),
                  pl.BlockSpec((1, S, LANES), lambda b, h, i: (b, 0, 0)),
                  pl.BlockSpec((hps, 3, LANES, LANES), lambda b, h, i: (h, 0, 0, 0))],
        out_specs=pl.BlockSpec((1, 1, hps * HEAD_DIM, TL), lambda b, h, i: (b, i, h, 0)),
        out_shape=jax.ShapeDtypeStruct((B, nT, FOX_W, TL), BF16),
        scratch_shapes=[pltpu.VMEM((hps, S, LANES), BF16), pltpu.VMEM((hps, KAUG, TL), BF16)] + _flash_scratch(hps),
        compiler_params=pltpu.CompilerParams(dimension_semantics=("arbitrary",) * 3, vmem_limit_bytes=VMEM_LIMIT),
        name="fox_attn",
    )(p["fqT"], p["fk"], p["fvT"], p["negc"], e_fox)


def _rank_rows(score, nrows):
    ridx = lax.broadcasted_iota(jnp.int32, score.shape, 0)
    rank = jnp.zeros(score.shape, F32)
    for n2 in range(nrows):
        row = score[n2:n2 + 1, :]
        ahead = (row > score) | ((row == score) & (n2 < ridx))
        rank = rank + jnp.where(ahead, 1.0, 0.0)
    return rank


def _moba_body(qT_ref, k_ref, vT_ref, kmean_ref, ptab_ref, o_ref, q2_ref, *flash, nbp, hps):
    bufs, (m_ref, l_ref, acc_ref) = flash[:hps], flash[hps:]
    i = pl.program_id(2)
    tq = TL
    zpad = jnp.zeros((HEAD_DIM, tq), BF16)
    blk = lax.broadcasted_iota(jnp.int32, (nbp, tq), 0)
    for h in range(hps):
        qT = qT_ref[0, 0, h * HEAD_DIM:(h + 1) * HEAD_DIM, :]
        gate = _dot(kmean_ref[0, :, h * LANES:(h + 1) * LANES].astype(BF16),
                    jnp.concatenate([qT, zpad], axis=0)) * (HEAD_DIM ** 0.5)
        gate = jnp.where(blk < i, gate, -FORCE_SCORE)
        sel = (_rank_rows(gate, nbp) < MOBA_TOPK) & (blk < i)
        selbias = jnp.where(sel | (blk == i), 0.0, NEG_INF)
        feat_rows = jnp.concatenate([selbias, _feature_rows(3, tq, 1.0), jnp.zeros((LANES - nbp - 16, tq), F32)], axis=0)
        q2_ref[h] = jnp.concatenate([qT, zpad, feat_rows.astype(BF16)], axis=0)

    def scores(j, h, t):
        k2 = jnp.concatenate([k_ref[0, pl.ds(j * TL, TL), h * LANES:(h + 1) * LANES],
                              ptab_ref[h, pl.ds(j * TL, TL), :]], axis=1)
        sT = _dot(k2, q2_ref[h])
        return jnp.where(_causal_keep(tq), sT, NEG_INF) if t is None else sT

    _flash_sweep(hps, i, i, lambda t: t, scores, lambda j, h: vT_ref[0, j, h * HEAD_DIM:(h + 1) * HEAD_DIM, :],
                 bufs, m_ref, l_ref, acc_ref)
    for h in range(hps):
        o_ref[0, 0, h * HEAD_DIM:(h + 1) * HEAD_DIM, :] = (acc_ref[h] / l_ref[h, 0:1, :]).astype(BF16)


def _moba_tables(S, nbp):
    pos = np.arange(S)
    onehot = (pos[:, None] // MOBA_BLOCK == np.arange(nbp)[None, :]).astype(np.float32)
    tabs = []
    for slope in _alibi_slopes(MOBA_HEADS):
        hi, lo, lo2 = _split3(jnp.asarray(slope * pos, F32))
        feat = jnp.stack([hi, lo, lo2], axis=1).astype(F32)
        tabs.append(jnp.concatenate([jnp.asarray(onehot), feat, jnp.zeros((S, LANES - nbp - 3), F32)], axis=1))
    return jnp.stack(tabs).astype(BF16)


def _moba_attention(p, hps=MOBA_HEADS):
    B, nT = p["mqT"].shape[:2]
    S = nT * TL
    nbp = -(-nT // 16) * 16
    kmean = jnp.pad(p["kmean"], ((0, 0), (0, nbp - nT), (0, 0)))
    ptab = _moba_tables(S, nbp)
    return pl.pallas_call(
        functools.partial(_moba_body, nbp=nbp, hps=hps),
        grid=(B, MOBA_HEADS // hps, nT),
        in_specs=[pl.BlockSpec((1, 1, hps * HEAD_DIM, TL), lambda b, h, i: (b, i, h, 0)),
                  pl.BlockSpec((1, S, hps * LANES), lambda b, h, i: (b, 0, h)),
                  pl.BlockSpec((1, nT, hps * HEAD_DIM, TL), lambda b, h, i: (b, 0, h, 0)),
                  pl.BlockSpec((1, nbp, hps * LANES), lambda b, h, i: (b, 0, h)),
                  pl.BlockSpec((hps, S, LANES), lambda b, h, i: (h, 0, 0))],
        out_specs=pl.BlockSpec((1, 1, hps * HEAD_DIM, TL), lambda b, h, i: (b, i, h, 0)),
        out_shape=jax.ShapeDtypeStruct((B, nT, MOBA_W, TL), BF16),
        scratch_shapes=[pltpu.VMEM((hps, KAUG, TL), BF16)] + _flash_scratch(hps),
        compiler_params=pltpu.CompilerParams(dimension_semantics=("arbitrary",) * 3, vmem_limit_bytes=VMEM_LIMIT),
        name="moba_attn",
    )(p["mqT"], p["mk"], p["mvT"], kmean, ptab)


def _cmp_body(kc_ref, vc_ref, w1_ref, b1_ref, w2k_ref, w2vT_ref, pos_ref, kcmp_ref, vcmpT_ref, pad_ref, *, S, nc):
    tail = pad_ref.shape[0] - S

    def hidden(src_ref, which):
        pad_ref[0:S, :] = src_ref[0]
        pad_ref[S:S + tail, :] = jnp.zeros((tail, LANES), F32)
        acc = jnp.zeros((nc, NSA_CMP_HIDDEN), F32)
        for l in range(NSA_CMP_LEN):
            rows = pad_ref[pl.ds(l, nc, stride=NSA_CMP_STRIDE), :] + pos_ref[which, l:l + 1, :]
            acc = acc + _dot(rows.astype(BF16), w1_ref[which, l])
        return jax.nn.gelu(acc + b1_ref[which]).astype(BF16)

    kcmp_ref[0] = _dot(hidden(kc_ref, 0), w2k_ref[...]).astype(BF16)
    vcmpT_ref[0] = _nt_dot(w2vT_ref[...], hidden(vc_ref, 1)).astype(BF16)


def _nsa_compress(p, cmp_w1, cmp_b1, cmp_w2, cmp_pos):
    B, S, _ = p["kc"].shape
    nc = S // NSA_CMP_STRIDE
    w1 = jnp.pad(cmp_w1, ((0, 0), (0, 0), (0, LANES - HEAD_DIM), (0, 0))).astype(BF16)
    pos = jnp.pad(cmp_pos, ((0, 0), (0, 0), (0, LANES - HEAD_DIM)))
    b1 = cmp_b1[:, None, :]
    w2k = jnp.pad(cmp_w2[0], ((0, 0), (0, LANES - HEAD_DIM))).astype(BF16)
    w2vT = cmp_w2[1].T.astype(BF16)
    full = lambda a: pl.BlockSpec(a.shape, lambda b: (0,) * a.ndim)
    return pl.pallas_call(
        functools.partial(_cmp_body, S=S, nc=nc),
        grid=(B,),
        in_specs=[pl.BlockSpec((1, S, LANES), lambda b: (b, 0, 0)), pl.BlockSpec((1, S, LANES), lambda b: (b, 0, 0)),
                  full(w1), full(b1), full(w2k), full(w2vT), full(pos)],
        out_specs=[pl.BlockSpec((1, nc, LANES), lambda b: (b, 0, 0)), pl.BlockSpec((1, HEAD_DIM, nc), lambda b: (b, 0, 0))],
        out_shape=[jax.ShapeDtypeStruct((B, nc, LANES), BF16), jax.ShapeDtypeStruct((B, HEAD_DIM, nc), BF16)],
        scratch_shapes=[pltpu.VMEM((S + NSA_CMP_LEN, LANES), F32)],
        compiler_params=pltpu.CompilerParams(dimension_semantics=("arbitrary",), vmem_limit_bytes=VMEM_LIMIT),
        name="nsa_compress",
    )(p["kc"], p["vc"], w1, b1, w2k, w2vT, pos)


def _nsa_body(qT_ref, kcmp_ref, vcmpT_ref, ctab_ref, aT_ref, ks_ref, kw_ref, vT_ref, gT_ref, stab_ref, wtab_ref,
              o_ref, score_ref, q2_ref, *flash, nc, nsel):
    bufs, (m_ref, l_ref, acc_ref) = flash[:NSA_HEADS], flash[NSA_HEADS:]
    i = pl.program_id(1)
    tq = TL
    t0 = i * tq
    slopes = _alibi_slopes(NSA_HEADS)
    zpad = jnp.zeros((HEAD_DIM, tq), BF16)

    def q2T_of(h, selbias):
        rows = jnp.concatenate([selbias, _feature_rows(3, tq, float(slopes[h])),
                                jnp.zeros((LANES - nsel - 16, tq), F32)], axis=0)
        return jnp.concatenate([qT_ref[0, 0, h * HEAD_DIM:(h + 1) * HEAD_DIM, :], zpad, rows.astype(BF16)], axis=0)

    kc2 = jnp.concatenate([kcmp_ref[0], ctab_ref[...]], axis=1)
    nn, qq = _tile_iotas(nc, tq)
    vis = nn * NSA_CMP_STRIDE + (NSA_CMP_LEN - 1) <= t0 + qq
    nosel = jnp.zeros((nsel, tq), F32)
    imp = jnp.zeros((nc, tq), F32)
    o_c = []
    for h in range(NSA_HEADS):
        sT = jnp.where(vis, _dot(kc2, q2T_of(h, nosel)), NEG_INF)
        m = jnp.max(sT, axis=0, keepdims=True)
        e = jnp.where(vis, jnp.exp(sT - m), 0.0)
        l = jnp.sum(e, axis=0, keepdims=True)
        pc = e / jnp.where(l > 0.0, l, 1.0)
        imp = imp + pc
        o_c.append(_dot(vcmpT_ref[0], pc.astype(BF16)))

    imp_sel = _dot(aT_ref[...], imp, precision=HIGHEST)
    jj, qs = _tile_iotas(nsel, tq)
    cur = (t0 + qs) // NSA_SEL_LEN
    causal = jj <= cur
    forced = (jj == 0) | (jj > cur - NSA_LOCAL)
    score = jnp.where(causal, jnp.where(forced, FORCE_SCORE, imp_sel), -FORCE_SCORE)
    score_ref[...] = score

    def rank_step(j2, rank):
        row = score_ref[pl.ds(j2, 1), :]
        ahead = (row > score) | ((row == score) & (j2 < jj))
        return rank + jnp.where(ahead, 1.0, 0.0)

    rank = lax.fori_loop(0, nsel, rank_step, jnp.zeros((nsel, tq), F32), unroll=8)
    selbias = jnp.where(rank < min(NSA_TOPN, nsel), 0.0, NEG_INF)

    kk, qk = _tile_iotas(TL, tq)

    def branch_scores(k_ref, tab_ref, oldest):
        def scores(j, h, t):
            k2 = jnp.concatenate([k_ref[0, pl.ds(j * TL, TL), :], tab_ref[pl.ds(j * TL, TL), :]], axis=1)
            sT = _dot(k2, q2_ref[h])
            if t is None:
                return jnp.where(kk <= qk, sT, NEG_INF)
            if oldest is not None:
                return jnp.where((t != oldest) | (kk > qk), sT, NEG_INF)
            return sT
        return scores

    for h in range(NSA_HEADS):
        q2_ref[h] = q2T_of(h, selbias)
    _flash_sweep(NSA_HEADS, i, i, lambda t: t, branch_scores(ks_ref, stab_ref, None),
                 lambda j, h: vT_ref[0, j, 0:HEAD_DIM, :], bufs, m_ref, l_ref, acc_ref)
    o_s = [acc_ref[h] / l_ref[h, 0:1, :] for h in range(NSA_HEADS)]

    for h in range(NSA_HEADS):
        q2_ref[h] = q2T_of(h, nosel)
    nprev = NSA_WINDOW // TL
    _flash_sweep(NSA_HEADS, i, jnp.minimum(i, nprev), lambda t: i - 1 - t,
                 branch_scores(kw_ref, wtab_ref, nprev - 1),
                 lambda j, h: vT_ref[0, j, HEAD_DIM:2 * HEAD_DIM, :], bufs, m_ref, l_ref, acc_ref)

    for h in range(NSA_HEADS):
        o_w = acc_ref[h] / l_ref[h, 0:1, :]
        g = gT_ref[0, 0]
        out = (g[h:h + 1, :] * o_c[h] + g[NSA_HEADS + h:NSA_HEADS + h + 1, :] * o_s[h]
               + g[2 * NSA_HEADS + h:2 * NSA_HEADS + h + 1, :] * o_w)
        o_ref[0, 0, h * HEAD_DIM:(h + 1) * HEAD_DIM, :] = out.astype(BF16)


def _nsa_tables(S, nc, nsel):
    pos = np.arange(S)
    onehot = jnp.asarray((pos[:, None] // NSA_SEL_LEN == np.arange(nsel)[None, :]).astype(np.float32))
    pfeat = jnp.stack(_split3(jnp.asarray(pos, F32)), axis=1).astype(F32)
    rest = jnp.zeros((S, LANES - nsel - 3), F32)
    stab = jnp.concatenate([onehot, pfeat, rest], axis=1).astype(BF16)
    wtab = jnp.concatenate([jnp.zeros_like(onehot), pfeat, rest], axis=1).astype(BF16)
    cend = np.arange(nc) * NSA_CMP_STRIDE + NSA_CMP_LEN - 1
    cfeat = jnp.stack(_split3(jnp.asarray(cend, F32)), axis=1).astype(F32)
    ctab = jnp.concatenate([jnp.zeros((nc, nsel), F32), cfeat, jnp.zeros((nc, LANES - nsel - 3), F32)], axis=1).astype(BF16)
    ratio = NSA_SEL_LEN // NSA_CMP_STRIDE
    span = NSA_CMP_LEN // NSA_CMP_STRIDE
    a = np.zeros((nsel, nc), np.float32)
    for j in range(nsel):
        for mm in range(ratio):
            for n2 in range(span):
                if j * ratio + mm + n2 < nc:
                    a[j, j * ratio + mm + n2] += 1.0
    return stab, wtab, ctab, jnp.asarray(a)


def _nsa_attention(p, kcmp, vcmpT):
    B, nT = p["nqT"].shape[:2]
    S = nT * TL
    nc = kcmp.shape[1]
    nsel = S // NSA_SEL_LEN
    stab, wtab, ctab, aT = _nsa_tables(S, nc, nsel)
    full = lambda a: pl.BlockSpec(a.shape, lambda b, i: (0,) * a.ndim)
    return pl.pallas_call(
        functools.partial(_nsa_body, nc=nc, nsel=nsel),
        grid=(B, nT),
        in_specs=[pl.BlockSpec((1, 1, NSA_W, TL), lambda b, i: (b, i, 0, 0)),
                  pl.BlockSpec((1, nc, LANES), lambda b, i: (b, 0, 0)),
                  pl.BlockSpec((1, HEAD_DIM, nc), lambda b, i: (b, 0, 0)),
                  full(ctab), full(aT),
                  pl.BlockSpec((1, S, LANES), lambda b, i: (b, 0, 0)),
                  pl.BlockSpec((1, S, LANES), lambda b, i: (b, 0, 0)),
                  pl.BlockSpec((1, nT, 2 * HEAD_DIM, TL), lambda b, i: (b, 0, 0, 0)),
                  pl.BlockSpec((1, 1, 16, TL), lambda b, i: (b, i, 0, 0)),
                  full(stab), full(wtab)],
        out_specs=pl.BlockSpec((1, 1, NSA_W, TL), lambda b, i: (b, i, 0, 0)),
        out_shape=jax.ShapeDtypeStruct((B, nT, NSA_W, TL), BF16),
        scratch_shapes=[pltpu.VMEM((nsel, TL), F32), pltpu.VMEM((NSA_HEADS, KAUG, TL), BF16)] + _flash_scratch(NSA_HEADS),
        compiler_params=pltpu.CompilerParams(dimension_semantics=("arbitrary",) * 2, vmem_limit_bytes=VMEM_LIMIT),
        name="nsa_attn",
    )(p["nqT"], kcmp, vcmpT, ctab, aT, p["ks"], p["kw"], p["nvT"], p["gT"], stab, wtab)


def _merge_body(x_ref, g_ref, ofT_ref, onT_ref, omT_ref, wfT_ref, wnT_ref, wmT_ref, wgT_ref, bg_ref, woT_ref, out_ref, outT_ref):
    x = x_ref[0]
    hb = _rms(x, g_ref[...]).astype(BF16)
    ys = (_dot(wfT_ref[...], ofT_ref[0, 0]), _dot(wnT_ref[...], onT_ref[0, 0]), _dot(wmT_ref[...], omT_ref[0, 0]))
    merged = jnp.zeros_like(ys[0])
    for r in range(N_BRANCHES):
        rows = slice(r * D_MODEL, (r + 1) * D_MODEL)
        gate = jax.nn.sigmoid(_nt_dot(wgT_ref[rows, :], hb) + bg_ref[rows, :])
        merged = merged + gate * ys[r]
    outT_ref[...] = _dot(woT_ref[...], merged.astype(BF16))
    out_ref[0] = x + outT_ref[...].T


def _merge(x, g, ofT, onT, omT, w_br_fox, w_br_nsa, w_br_moba, w_gate, b_gate, w_out):
    B, S, D = x.shape
    nT = S // TL
    wfT, wnT, wmT = w_br_fox.T.astype(BF16), w_br_nsa.T.astype(BF16), w_br_moba.T.astype(BF16)
    wgT, bg, woT = w_gate.T.astype(BF16), b_gate[:, None], w_out.T.astype(BF16)
    full = lambda a: pl.BlockSpec(a.shape, lambda b, i: (0,) * a.ndim)
    tiled = lambda n: pl.BlockSpec((1, 1, n, TL), lambda b, i: (b, i, 0, 0))
    return pl.pallas_call(
        _merge_body,
        grid=(B, nT),
        in_specs=[pl.BlockSpec((1, TL, D), lambda b, i: (b, i, 0)), full(g), tiled(FOX_W), tiled(NSA_W), tiled(MOBA_W),
                  full(wfT), full(wnT), full(wmT), full(wgT), full(bg), full(woT)],
        out_specs=pl.BlockSpec((1, TL, D), lambda b, i: (b, i, 0)),
        out_shape=jax.ShapeDtypeStruct((B, S, D), F32),
        scratch_shapes=[pltpu.VMEM((D, TL), F32)],
        compiler_params=pltpu.CompilerParams(dimension_semantics=("arbitrary",) * 2, vmem_limit_bytes=VMEM_LIMIT),
        name="merge",
    )(x, g, ofT, onT, omT, wfT, wnT, wmT, wgT, bg, woT)


def _route(lg):
    def softmax_rows(rows):
        m = functools.reduce(jnp.maximum, rows)
        es = [jnp.exp(r - m) for r in rows]
        tot = functools.reduce(lambda a, b: a + b, es)
        return [e / tot for e in es]

    def rank_of(vals, n):
        r = jnp.zeros_like(vals[0])
        for n2 in range(len(vals)):
            if n2 == n:
                continue
            ahead = (vals[n2] > vals[n]) | ((vals[n2] == vals[n]) & (n2 < n))
            r = r + jnp.where(ahead, 1.0, 0.0)
        return r

    gl = [lg[r:r + 1, :] for r in range(N_GROUPS)]
    gp = softmax_rows(gl)
    out = []
    for gi in range(N_GROUPS):
        gw = jnp.where(rank_of(gl, gi) < 1.0, gp[gi], 0.0)
        ep = softmax_rows([lg[8 + gi * EXPERTS_PER_GROUP + e: 9 + gi * EXPERTS_PER_GROUP + e, :]
                           for e in range(EXPERTS_PER_GROUP)])
        top = [jnp.where(rank_of(ep, e) < float(TOPK_IN_GROUP), ep[e], 0.0) for e in range(EXPERTS_PER_GROUP)]
        tot = functools.reduce(lambda a, b: a + b, top)
        out.extend([gw * (t / tot) for t in top])
    return jnp.concatenate(out, axis=0)


def _moe_body(x_ref, g_ref, wr_ref, br_ref, wg_ref, wu_ref, wd_ref, gf_ref, out_ref, hb_ref, comb_ref, *, final):
    e = pl.program_id(1)
    tm = x_ref.shape[0]

    @pl.when(e == 0)
    def _():
        x = x_ref[...]
        h = _rms(x, g_ref[...])
        hb_ref[...] = h.astype(BF16)
        combT = _route(_nt_dot(wr_ref[...], h, precision=HIGHEST) + br_ref[...])
        comb = jnp.concatenate([combT, jnp.zeros((LANES - N_EXPERTS, tm), F32)], axis=0).T
        for k in range(N_EXPERTS):
            comb_ref[k] = jnp.broadcast_to(comb[:, k:k + 1], (tm, LANES))
        out_ref[...] = x

    hb = hb_ref[...]
    a = _dot(hb, wg_ref[0])
    u = _dot(hb, wu_ref[0])
    c = comb_ref[e]
    hid = (a * jax.nn.sigmoid(a)) * u * jnp.concatenate([c] * (EXPERT_HIDDEN // LANES), axis=1)
    out_ref[...] += _dot(hid.astype(BF16), wd_ref[0])

    if final:
        @pl.when(e == N_EXPERTS - 1)
        def _():
            out_ref[...] = _rms(out_ref[...], gf_ref[...])


def _moe(x, g, w_route_grp, b_route_grp, w_route_exp, b_route_exp, w_exp_gate, w_exp_up, w_exp_down, g_final, final, tm=1024):
    B, S, D = x.shape
    T = B * S
    xt = x.reshape(T, D)
    wr = jnp.zeros((32, D), F32).at[0:N_GROUPS].set(w_route_grp.T).at[8:8 + N_EXPERTS].set(w_route_exp.T)
    br = jnp.zeros((32, 1), F32).at[0:N_GROUPS, 0].set(b_route_grp).at[8:8 + N_EXPERTS, 0].set(b_route_exp)
    wg, wu, wd = w_exp_gate.astype(BF16), w_exp_up.astype(BF16), w_exp_down.astype(BF16)
    full = lambda a: pl.BlockSpec(a.shape, lambda t, e: (0,) * a.ndim)
    out = pl.pallas_call(
        functools.partial(_moe_body, final=final),
        grid=(T // tm, N_EXPERTS),
        in_specs=[pl.BlockSpec((tm, D), lambda t, e: (t, 0)), full(g), full(wr), full(br),
                  pl.BlockSpec((1, D, EXPERT_HIDDEN), lambda t, e: (e, 0, 0)),
                  pl.BlockSpec((1, D, EXPERT_HIDDEN), lambda t, e: (e, 0, 0)),
                  pl.BlockSpec((1, EXPERT_HIDDEN, D), lambda t, e: (e, 0, 0)),
                  full(g_final)],
        out_specs=pl.BlockSpec((tm, D), lambda t, e: (t, 0)),
        out_shape=jax.ShapeDtypeStruct((T, D), F32),
        scratch_shapes=[pltpu.VMEM((tm, D), BF16), pltpu.VMEM((N_EXPERTS, tm, LANES), F32)],
        compiler_params=pltpu.CompilerParams(dimension_semantics=("arbitrary",) * 2, vmem_limit_bytes=VMEM_LIMIT),
        name="moe",
    )(xt, g, wr, br, wg, wu, wd, g_final)
    return out.reshape(B, S, D)


def _fox_placement():
    e = np.zeros((FOX_HEADS, 3, LANES, LANES), np.float32)
    for h in range(FOX_HEADS):
        for j in range(3):
            e[h, j, h, j] = 1.0
    return jnp.asarray(e, BF16)


def kernel(x, norm_mix_g, norm_ffn_g, final_norm_g, w_in, b_in, cmp_w1, cmp_b1, cmp_w2, cmp_pos, w_br_fox, w_br_nsa, w_br_moba, w_out, w_route_grp, b_route_grp, w_route_exp, b_route_exp, w_exp_gate, w_exp_up, w_exp_down):
    depth = w_in.shape[0]
    e_fox = _fox_placement()
    gf = final_norm_g[None, :]
    for l in range(depth):
        g_mix = norm_mix_g[l][None, :]
        ws, bs, wt, bt = _proj_weights(w_in[l], b_in[l])
        p = _project(x, g_mix, ws, bs, wt, bt)
        ofT = _fox_attention(p, e_fox)
        kcmp, vcmpT = _nsa_compress(p, cmp_w1[l], cmp_b1[l], cmp_w2[l], cmp_pos[l])
        onT = _nsa_attention(p, kcmp, vcmpT)
        omT = _moba_attention(p)
        x = _merge(x, g_mix, ofT, onT, omT, w_br_fox[l], w_br_nsa[l], w_br_moba[l],
                   w_in[l][:, _O_ML:_O_END], b_in[l][_O_ML:_O_END], w_out[l])
        x = _moe(x, norm_ffn_g[l][None, :], w_route_grp[l], b_route_grp[l], w_route_exp[l], b_route_exp[l],
                 w_exp_gate[l], w_exp_up[l], w_exp_down[l], gf, final=(l == depth - 1))
    return x
```

```python
import functools

import numpy as np
import jax
import jax.numpy as jnp
from jax import lax
from jax.experimental import pallas as pl
from jax.experimental.pallas import tpu as pltpu

F32 = jnp.float32
BF16 = jnp.bfloat16
HIGHEST = lax.Precision.HIGHEST

D_MODEL = 1024
HEAD_DIM = 64
FOX_HEADS = 6
NSA_HEADS = 4
MOBA_HEADS = 6
FOX_W = FOX_HEADS * HEAD_DIM
NSA_W = NSA_HEADS * HEAD_DIM
MOBA_W = MOBA_HEADS * HEAD_DIM
N_BRANCHES = 3

NSA_CMP_LEN = 32
NSA_CMP_STRIDE = 16
NSA_CMP_HIDDEN = 256
NSA_SEL_LEN = 64
NSA_TOPN = 16
NSA_LOCAL = 2
NSA_WINDOW = 512
MOBA_BLOCK = 256
MOBA_TOPK = 3

N_GROUPS = 4
EXPERTS_PER_GROUP = 4
N_EXPERTS = N_GROUPS * EXPERTS_PER_GROUP
TOPK_IN_GROUP = 2
EXPERT_HIDDEN = 256

RMS_EPS = 1e-6
NEG_INF = -1e30
FORCE_SCORE = 1e9

LANES = 128
TL = 256
KAUG = 2 * LANES
VMEM_LIMIT = 56 * 1024 * 1024

_O_FQ = 0
_O_FK = _O_FQ + FOX_W
_O_FV = _O_FK + FOX_W
_O_FF = _O_FV + FOX_W
_O_NQ = _O_FF + FOX_HEADS
_O_KC = _O_NQ + NSA_W
_O_VC = _O_KC + HEAD_DIM
_O_KS = _O_VC + HEAD_DIM
_O_VS = _O_KS + HEAD_DIM
_O_KW = _O_VS + HEAD_DIM
_O_VW = _O_KW + HEAD_DIM
_O_NG = _O_VW + HEAD_DIM
_O_MQ = _O_NG + N_BRANCHES * NSA_HEADS
_O_MK = _O_MQ + MOBA_W
_O_MV = _O_MK + MOBA_W
_O_ML = _O_MV + MOBA_W
_O_END = _O_ML + N_BRANCHES * D_MODEL

_S_FK = 0
_S_MK = _S_FK + FOX_HEADS * LANES
_S_KS = _S_MK + MOBA_HEADS * LANES
_S_KW = _S_KS + LANES
_S_KC = _S_KW + LANES
_S_VC = _S_KC + LANES
_S_FF = _S_VC + LANES
_S_END = _S_FF + LANES
_T_FQ = 0
_T_FV = _T_FQ + FOX_W
_T_MQ = _T_FV + FOX_W
_T_MV = _T_MQ + MOBA_W
_T_NQ = _T_MV + MOBA_W
_T_NV = _T_NQ + NSA_W
_T_NG = _T_NV + 2 * HEAD_DIM
_T_END = _T_NG + 16

LOG2E = float(np.log2(np.e))
Q_SCALE = HEAD_DIM ** -0.5 * LOG2E


def _alibi_slopes(n):
    return 2.0 ** (-8.0 * np.arange(1, n + 1) / n)


def _split3(v):
    hi = v.astype(BF16)
    r1 = v - hi.astype(F32)
    lo = r1.astype(BF16)
    r2 = r1 - lo.astype(F32)
    return hi, lo, r2.astype(BF16)


def _nt_dot(a, b, precision=None):
    return lax.dot_general(a, b, (((1,), (1,)), ((), ())), preferred_element_type=F32, precision=precision)


def _dot(a, b, precision=None):
    return jnp.dot(a, b, preferred_element_type=F32, precision=precision)


def _rms(x, g):
    return x * lax.rsqrt(jnp.mean(x * x, axis=-1, keepdims=True) + RMS_EPS) * g


def _proj_body(x_ref, g_ref, ws_ref, bs_ref, wt_ref, bt_ref,
               fk_ref, mk_ref, ks_ref, kw_ref, kc_ref, vc_ref, negc_ref, kmean_ref,
               fqT_ref, fvT_ref, mqT_ref, mvT_ref, nqT_ref, nvT_ref, gT_ref, carry_ref, *, tm):
    i = pl.program_id(1)
    hb = _rms(x_ref[0], g_ref[...]).astype(BF16)

    def sdot(a, b):
        return _dot(hb, ws_ref[:, a:b]) + bs_ref[:, a:b]

    def tdot(a, b):
        return _nt_dot(wt_ref[a:b, :], hb) + bt_ref[a:b, :]

    def put_t(ref, val):
        for c in range(tm // TL):
            ref[0, c] = val[:, c * TL:(c + 1) * TL]

    fk_ref[0] = sdot(_S_FK, _S_MK).astype(BF16)
    mk = sdot(_S_MK, _S_KS)
    mk_ref[0] = mk.astype(BF16)
    nblk = tm // MOBA_BLOCK
    means = [jnp.mean(mk[c * MOBA_BLOCK:(c + 1) * MOBA_BLOCK, :], axis=0, keepdims=True) for c in range(nblk)]
    kmean_ref[0, 0] = jnp.concatenate(means + [jnp.zeros((8 - nblk, mk.shape[1]), F32)], axis=0)
    ks_ref[0] = sdot(_S_KS, _S_KW).astype(BF16)
    kw_ref[0] = sdot(_S_KW, _S_KC).astype(BF16)
    kc_ref[0] = sdot(_S_KC, _S_VC)
    vc_ref[0] = sdot(_S_VC, _S_FF)

    @pl.when(i == 0)
    def _():
        carry_ref[...] = jnp.zeros_like(carry_ref)

    fl = sdot(_S_FF, _S_END)
    ls = jnp.minimum(fl, 0.0) - jnp.log1p(jnp.exp(-jnp.abs(fl)))
    ls = jnp.where(lax.broadcasted_iota(jnp.int32, ls.shape, 1) < FOX_HEADS, ls, 0.0)
    tri = (lax.broadcasted_iota(jnp.int32, (tm, tm), 1) <= lax.broadcasted_iota(jnp.int32, (tm, tm), 0)).astype(F32)
    cs = _dot(tri, ls, precision=HIGHEST) + carry_ref[0:1, :]
    negc_ref[0] = -cs
    carry_ref[0:1, :] = cs[tm - 1:tm, :]

    put_t(fqT_ref, tdot(_T_FQ, _T_FV).astype(BF16))
    put_t(fvT_ref, tdot(_T_FV, _T_MQ).astype(BF16))
    put_t(mqT_ref, tdot(_T_MQ, _T_MV).astype(BF16))
    put_t(mvT_ref, tdot(_T_MV, _T_NQ).astype(BF16))
    put_t(nqT_ref, tdot(_T_NQ, _T_NV).astype(BF16))
    put_t(nvT_ref, tdot(_T_NV, _T_NG).astype(BF16))
    put_t(gT_ref, jax.nn.sigmoid(tdot(_T_NG, _T_END)))


def _proj_weights(w, b):
    def pad_heads(cols, nh):
        m = cols.reshape(cols.shape[0], nh, HEAD_DIM)
        return jnp.pad(m, ((0, 0), (0, 0), (0, LANES - HEAD_DIM))).reshape(cols.shape[0], nh * LANES)

    def pad_to(cols, n):
        return jnp.pad(cols, ((0, 0), (0, n - cols.shape[1])))

    wb = jnp.concatenate([w, b[None, :]], axis=0)
    sl = lambda o, n: wb[:, o:o + n]
    std = jnp.concatenate([
        pad_heads(sl(_O_FK, FOX_W), FOX_HEADS), pad_heads(sl(_O_MK, MOBA_W), MOBA_HEADS),
        pad_to(sl(_O_KS, HEAD_DIM), LANES), pad_to(sl(_O_KW, HEAD_DIM), LANES),
        pad_to(sl(_O_KC, HEAD_DIM), LANES), pad_to(sl(_O_VC, HEAD_DIM), LANES),
        pad_to(sl(_O_FF, FOX_HEADS), LANES)], axis=1)
    scale = Q_SCALE
    tr = jnp.concatenate([
        sl(_O_FQ, FOX_W) * scale, sl(_O_FV, FOX_W), sl(_O_MQ, MOBA_W) * scale, sl(_O_MV, MOBA_W),
        sl(_O_NQ, NSA_W) * scale, sl(_O_VS, HEAD_DIM), sl(_O_VW, HEAD_DIM),
        pad_to(sl(_O_NG, N_BRANCHES * NSA_HEADS), 16)], axis=1)
    ws, bs = std[:-1].astype(BF16), std[-1:].astype(F32)
    wt, bt = tr[:-1].T.astype(BF16), tr[-1:].T.astype(F32)
    return ws, bs, wt, bt


def _project(x, g, ws, bs, wt, bt, tm=512):
    B, S, D = x.shape
    nT = S // TL
    nt = tm // TL
    grid = (B, S // tm)
    row = lambda n, dt: jax.ShapeDtypeStruct((B, S, n), dt)
    tiled = lambda n, dt: jax.ShapeDtypeStruct((B, nT, n, TL), dt)
    row_spec = lambda n: pl.BlockSpec((1, tm, n), lambda b, i: (b, i, 0))
    tiled_spec = lambda n: pl.BlockSpec((1, nt, n, TL), lambda b, i: (b, i, 0, 0))
    full = lambda a: pl.BlockSpec(a.shape, lambda b, i: (0,) * a.ndim)
    out_shape = [row(FOX_HEADS * LANES, BF16), row(MOBA_HEADS * LANES, BF16), row(LANES, BF16), row(LANES, BF16),
                 row(LANES, F32), row(LANES, F32), row(LANES, F32),
                 jax.ShapeDtypeStruct((B, S // tm, 8, MOBA_HEADS * LANES), F32),
                 tiled(FOX_W, BF16), tiled(FOX_W, BF16), tiled(MOBA_W, BF16), tiled(MOBA_W, BF16),
                 tiled(NSA_W, BF16), tiled(2 * HEAD_DIM, BF16), tiled(16, F32)]
    out_specs = [row_spec(FOX_HEADS * LANES), row_spec(MOBA_HEADS * LANES), row_spec(LANES), row_spec(LANES),
                 row_spec(LANES), row_spec(LANES), row_spec(LANES),
                 pl.BlockSpec((1, 1, 8, MOBA_HEADS * LANES), lambda b, i: (b, i, 0, 0)),
                 tiled_spec(FOX_W), tiled_spec(FOX_W), tiled_spec(MOBA_W), tiled_spec(MOBA_W),
                 tiled_spec(NSA_W), tiled_spec(2 * HEAD_DIM), tiled_spec(16)]
    outs = pl.pallas_call(
        functools.partial(_proj_body, tm=tm),
        grid=grid,
        in_specs=[pl.BlockSpec((1, tm, D), lambda b, i: (b, i, 0)), full(g), full(ws), full(bs), full(wt), full(bt)],
        out_specs=out_specs,
        out_shape=out_shape,
        scratch_shapes=[pltpu.VMEM((8, LANES), F32)],
        compiler_params=pltpu.CompilerParams(dimension_semantics=("arbitrary", "arbitrary"),
                                             vmem_limit_bytes=VMEM_LIMIT),
        name="proj",
    )(x, g, ws, bs, wt, bt)
    (fk, mk, ks, kw, kc, vc, negc, kmean8, fqT, fvT, mqT, mvT, nqT, nvT, gT) = outs
    nblk = tm // MOBA_BLOCK
    kmean = kmean8[:, :, :nblk, :].reshape(B, (S // tm) * nblk, MOBA_HEADS * LANES)
    return dict(fk=fk, mk=mk, ks=ks, kw=kw, kc=kc, vc=vc, negc=negc, kmean=kmean,
                fqT=fqT, fvT=fvT, mqT=mqT, mvT=mvT, nqT=nqT, nvT=nvT, gT=gT)


def _flash_sweep(n_heads, diag_tile, n_rest, rest_tile, scores, value_tile, bufs, m_ref, l_ref, acc_ref):
    half = n_heads // 2
    groups = (tuple(range(half)), tuple(range(half, n_heads)))
    for h in range(n_heads):
        m_ref[h] = jnp.full(m_ref.shape[1:], NEG_INF, F32)
        l_ref[h] = jnp.zeros(l_ref.shape[1:], F32)
        acc_ref[h] = jnp.zeros(acc_ref.shape[1:], F32)

    def produce(j, grp, t):
        mts = []
        for h in grp:
            sT = scores(j, h, t)
            bufs[h][...] = sT
            mts.append(jnp.max(sT, axis=0, keepdims=True))
        return tuple(mts)

    def consume(j, grp, mts):
        for h, mt in zip(grp, mts):
            m = m_ref[h, 0:1, :]
            m_new = jnp.maximum(m, mt)
            p = jnp.exp2(bufs[h][...] - m_new)
            alpha = jnp.exp2(m - m_new)
            m_ref[h, 0:1, :] = m_new
            l_ref[h, 0:1, :] = alpha * l_ref[h, 0:1, :] + jnp.sum(p, axis=0, keepdims=True)
            acc_ref[h] = alpha * acc_ref[h] + _dot(value_tile(j, h), p.astype(BF16))

    def tile_steps(j, t, prev):
        mts0 = produce(j, groups[0], t)
        if prev is not None:
            consume(prev[0], groups[1], prev[1])
        mts1 = produce(j, groups[1], t)
        consume(j, groups[0], mts0)
        return mts1

    def step(t, mts):
        prev_tile = jnp.where(t == 0, diag_tile, rest_tile(t - 1))
        return tile_steps(rest_tile(t), t, (prev_tile, mts))

    mts = lax.fori_loop(0, n_rest // 2, lambda u, mts: step(2 * u + 1, step(2 * u, mts)),
                        tile_steps(diag_tile, None, None))
    mts = lax.cond(n_rest % 2 == 1, lambda mts: step(n_rest - 1, mts), lambda mts: mts, mts)
    consume(jnp.where(n_rest == 0, diag_tile, rest_tile(n_rest - 1)), groups[1], mts)


def _tile_iotas(tk, tq):
    kk = lax.broadcasted_iota(jnp.int32, (tk, tq), 0)
    qq = lax.broadcasted_iota(jnp.int32, (tk, tq), 1)
    return kk, qq


def _feature_rows(nrows, tq, value):
    r = lax.broadcasted_iota(jnp.int32, (16, tq), 0)
    return jnp.where(r < nrows, value, 0.0).astype(F32)


def _causal_keep(tq):
    kk, qq = _tile_iotas(TL, tq)
    return kk <= qq


def _flash_scratch(n_heads):
    return ([pltpu.VMEM((TL, TL), F32) for _ in range(n_heads)]
            + [pltpu.VMEM((n_heads, 8, TL), F32), pltpu.VMEM((n_heads, 8, TL), F32),
               pltpu.VMEM((n_heads, HEAD_DIM, TL), F32)])


def _fox_body(qT_ref, k_ref, vT_ref, negc_ref, e_ref, o_ref, kfeat_ref, q2_ref, *flash, S, hps):
    bufs, (m_ref, l_ref, acc_ref) = flash[:hps], flash[hps:]
    i = pl.program_id(2)
    tq = TL

    @pl.when(i == 0)
    def _():
        chunk = 512
        for c in range(S // chunk):
            hi, lo, lo2 = _split3(negc_ref[0, c * chunk:(c + 1) * chunk, :] * LOG2E)
            for h in range(hps):
                feat = _dot(hi, e_ref[h, 0]) + _dot(lo, e_ref[h, 1]) + _dot(lo2, e_ref[h, 2])
                kfeat_ref[h, c * chunk:(c + 1) * chunk, :] = feat.astype(BF16)

    feat_rows = jnp.concatenate([_feature_rows(3, tq, 1.0), jnp.zeros((LANES - 16, tq), F32)], axis=0).astype(BF16)
    for h in range(hps):
        q2_ref[h] = jnp.concatenate([qT_ref[0, 0, h * HEAD_DIM:(h + 1) * HEAD_DIM, :],
                                     jnp.zeros((HEAD_DIM, tq), BF16), feat_rows], axis=0)

    def scores(j, h, t):
        k2 = jnp.concatenate([k_ref[0, pl.ds(j * TL, TL), h * LANES:(h + 1) * LANES],
                              kfeat_ref[h, pl.ds(j * TL, TL), :]], axis=1)
        sT = _dot(k2, q2_ref[h])
        return jnp.where(_causal_keep(tq), sT, NEG_INF) if t is None else sT

    _flash_sweep(hps, i, i, lambda t: t, scores, lambda j, h: vT_ref[0, j, h * HEAD_DIM:(h + 1) * HEAD_DIM, :],
                 bufs, m_ref, l_ref, acc_ref)
    for h in range(hps):
        o_ref[0, 0, h * HEAD_DIM:(h + 1) * HEAD_DIM, :] = (acc_ref[h] / l_ref[h, 0:1, :]).astype(BF16)


def _fox_attention(p, e_fox, hps=FOX_HEADS):
    B, nT = p["fqT"].shape[:2]
    S = nT * TL
    return pl.pallas_call(
        functools.partial(_fox_body, S=S, hps=hps),
        grid=(B, FOX_HEADS // hps, nT),
        in_specs=[pl.BlockSpec((1, 1, hps * HEAD_DIM, TL), lambda b, h, i: (b, i, h, 0)),
                  pl.BlockSpec((1, S, hps * LANES), lambda b, h, i: (b, 0, h)),
                  pl.BlockSpec((1, nT, hps * HEAD_DIM, TL), lambda b, h, i: (b, 0, h, 0)),
                  pl.BlockSpec((1, S, LANES), lambda b, h, i: (b, 0, 0)),
                  pl.BlockSpec((hps, 3, LANES, LANES), lambda b, h, i: (h, 0, 0, 0))],
        out_specs=pl.BlockSpec((1, 1, hps * HEAD_DIM, TL), lambda b, h, i: (b, i, h, 0)),
        out_shape=jax.ShapeDtypeStruct((B, nT, FOX_W, TL), BF16),
        scratch_shapes=[pltpu.VMEM((hps, S, LANES), BF16), pltpu.VMEM((hps, KAUG, TL), BF16)] + _flash_scratch(hps),
        compiler_params=pltpu.CompilerParams(dimension_semantics=("arbitrary",) * 3, vmem_limit_bytes=VMEM_LIMIT),
        name="fox_attn",
    )(p["fqT"], p["fk"], p["fvT"], p["negc"], e_fox)


def _rank_rows(score, nrows):
    ridx = lax.broadcasted_iota(jnp.int32, score.shape, 0)
    rank = jnp.zeros(score.shape, F32)
    for n2 in range(nrows):
        row = score[n2:n2 + 1, :]
        ahead = (row > score) | ((row == score) & (n2 < ridx))
        rank = rank + jnp.where(ahead, 1.0, 0.0)
    return rank


def _moba_body(qT_ref, k_ref, vT_ref, kmean_ref, ptab_ref, o_ref, q2_ref, *flash, nbp, hps):
    bufs, (m_ref, l_ref, acc_ref) = flash[:hps], flash[hps:]
    i = pl.program_id(2)
    tq = TL
    zpad = jnp.zeros((HEAD_DIM, tq), BF16)
    blk = lax.broadcasted_iota(jnp.int32, (nbp, tq), 0)
    for h in range(hps):
        qT = qT_ref[0, 0, h * HEAD_DIM:(h + 1) * HEAD_DIM, :]
        gate = _dot(kmean_ref[0, :, h * LANES:(h + 1) * LANES].astype(BF16),
                    jnp.concatenate([qT, zpad], axis=0)) * (1.0 / Q_SCALE)
        gate = jnp.where(blk < i, gate, -FORCE_SCORE)
        sel = (_rank_rows(gate, nbp) < MOBA_TOPK) & (blk < i)
        selbias = jnp.where(sel | (blk == i), 0.0, NEG_INF)
        feat_rows = jnp.concatenate([selbias, _feature_rows(3, tq, 1.0), jnp.zeros((LANES - nbp - 16, tq), F32)], axis=0)
        q2_ref[h] = jnp.concatenate([qT, zpad, feat_rows.astype(BF16)], axis=0)

    def scores(j, h, t):
        k2 = jnp.concatenate([k_ref[0, pl.ds(j * TL, TL), h * LANES:(h + 1) * LANES],
                              ptab_ref[h, pl.ds(j * TL, TL), :]], axis=1)
        sT = _dot(k2, q2_ref[h])
        return jnp.where(_causal_keep(tq), sT, NEG_INF) if t is None else sT

    _flash_sweep(hps, i, i, lambda t: t, scores, lambda j, h: vT_ref[0, j, h * HEAD_DIM:(h + 1) * HEAD_DIM, :],
                 bufs, m_ref, l_ref, acc_ref)
    for h in range(hps):
        o_ref[0, 0, h * HEAD_DIM:(h + 1) * HEAD_DIM, :] = (acc_ref[h] / l_ref[h, 0:1, :]).astype(BF16)


def _moba_tables(S, nbp):
    pos = np.arange(S)
    onehot = (pos[:, None] // MOBA_BLOCK == np.arange(nbp)[None, :]).astype(np.float32)
    tabs = []
    for slope in _alibi_slopes(MOBA_HEADS):
        hi, lo, lo2 = _split3(jnp.asarray(slope * LOG2E * pos, F32))
        feat = jnp.stack([hi, lo, lo2], axis=1).astype(F32)
        tabs.append(jnp.concatenate([jnp.asarray(onehot), feat, jnp.zeros((S, LANES - nbp - 3), F32)], axis=1))
    return jnp.stack(tabs).astype(BF16)


def _moba_attention(p, hps=MOBA_HEADS):
    B, nT = p["mqT"].shape[:2]
    S = nT * TL
    nbp = -(-nT // 16) * 16
    kmean = jnp.pad(p["kmean"], ((0, 0), (0, nbp - nT), (0, 0)))
    ptab = _moba_tables(S, nbp)
    return pl.pallas_call(
        functools.partial(_moba_body, nbp=nbp, hps=hps),
        grid=(B, MOBA_HEADS // hps, nT),
        in_specs=[pl.BlockSpec((1, 1, hps * HEAD_DIM, TL), lambda b, h, i: (b, i, h, 0)),
                  pl.BlockSpec((1, S, hps * LANES), lambda b, h, i: (b, 0, h)),
                  pl.BlockSpec((1, nT, hps * HEAD_DIM, TL), lambda b, h, i: (b, 0, h, 0)),
                  pl.BlockSpec((1, nbp, hps * LANES), lambda b, h, i: (b, 0, h)),
                  pl.BlockSpec((hps, S, LANES), lambda b, h, i: (h, 0, 0))],
        out_specs=pl.BlockSpec((1, 1, hps * HEAD_DIM, TL), lambda b, h, i: (b, i, h, 0)),
        out_shape=jax.ShapeDtypeStruct((B, nT, MOBA_W, TL), BF16),
        scratch_shapes=[pltpu.VMEM((hps, KAUG, TL), BF16)] + _flash_scratch(hps),
        compiler_params=pltpu.CompilerParams(dimension_semantics=("arbitrary",) * 3, vmem_limit_bytes=VMEM_LIMIT),
        name="moba_attn",
    )(p["mqT"], p["mk"], p["mvT"], kmean, ptab)


def _cmp_body(kc_ref, vc_ref, w1_ref, b1_ref, w2k_ref, w2vT_ref, pos_ref, kcmp_ref, vcmpT_ref, pad_ref, *, S, nc):
    tail = pad_ref.shape[0] - S

    def hidden(src_ref, which):
        pad_ref[0:S, :] = src_ref[0]
        pad_ref[S:S + tail, :] = jnp.zeros((tail, LANES), F32)
        acc = jnp.zeros((nc, NSA_CMP_HIDDEN), F32)
        for l in range(NSA_CMP_LEN):
            rows = pad_ref[pl.ds(l, nc, stride=NSA_CMP_STRIDE), :] + pos_ref[which, l:l + 1, :]
            acc = acc + _dot(rows.astype(BF16), w1_ref[which, l])
        return jax.nn.gelu(acc + b1_ref[which]).astype(BF16)

    kcmp_ref[0] = _dot(hidden(kc_ref, 0), w2k_ref[...]).astype(BF16)
    vcmpT_ref[0] = _nt_dot(w2vT_ref[...], hidden(vc_ref, 1)).astype(BF16)


def _nsa_compress(p, cmp_w1, cmp_b1, cmp_w2, cmp_pos):
    B, S, _ = p["kc"].shape
    nc = S // NSA_CMP_STRIDE
    w1 = jnp.pad(cmp_w1, ((0, 0), (0, 0), (0, LANES - HEAD_DIM), (0, 0))).astype(BF16)
    pos = jnp.pad(cmp_pos, ((0, 0), (0, 0), (0, LANES - HEAD_DIM)))
    b1 = cmp_b1[:, None, :]
    w2k = jnp.pad(cmp_w2[0], ((0, 0), (0, LANES - HEAD_DIM))).astype(BF16)
    w2vT = cmp_w2[1].T.astype(BF16)
    full = lambda a: pl.BlockSpec(a.shape, lambda b: (0,) * a.ndim)
    return pl.pallas_call(
        functools.partial(_cmp_body, S=S, nc=nc),
        grid=(B,),
        in_specs=[pl.BlockSpec((1, S, LANES), lambda b: (b, 0, 0)), pl.BlockSpec((1, S, LANES), lambda b: (b, 0, 0)),
                  full(w1), full(b1), full(w2k), full(w2vT), full(pos)],
        out_specs=[pl.BlockSpec((1, nc, LANES), lambda b: (b, 0, 0)), pl.BlockSpec((1, HEAD_DIM, nc), lambda b: (b, 0, 0))],
        out_shape=[jax.ShapeDtypeStruct((B, nc, LANES), BF16), jax.ShapeDtypeStruct((B, HEAD_DIM, nc), BF16)],
        scratch_shapes=[pltpu.VMEM((S + NSA_CMP_LEN, LANES), F32)],
        compiler_params=pltpu.CompilerParams(dimension_semantics=("arbitrary",), vmem_limit_bytes=VMEM_LIMIT),
        name="nsa_compress",
    )(p["kc"], p["vc"], w1, b1, w2k, w2vT, pos)


def _nsa_body(qT_ref, kcmp_ref, vcmpT_ref, ctab_ref, aT_ref, ks_ref, kw_ref, vT_ref, gT_ref, stab_ref, wtab_ref,
              o_ref, score_ref, q2_ref, *flash, nc, nsel):
    bufs, (m_ref, l_ref, acc_ref) = flash[:NSA_HEADS], flash[NSA_HEADS:]
    i = pl.program_id(1)
    tq = TL
    t0 = i * tq
    slopes = _alibi_slopes(NSA_HEADS)
    zpad = jnp.zeros((HEAD_DIM, tq), BF16)

    def q2T_of(h, selbias):
        rows = jnp.concatenate([selbias, _feature_rows(3, tq, float(slopes[h])),
                                jnp.zeros((LANES - nsel - 16, tq), F32)], axis=0)
        return jnp.concatenate([qT_ref[0, 0, h * HEAD_DIM:(h + 1) * HEAD_DIM, :], zpad, rows.astype(BF16)], axis=0)

    kc2 = jnp.concatenate([kcmp_ref[0], ctab_ref[...]], axis=1)
    nn, qq = _tile_iotas(nc, tq)
    vis = nn * NSA_CMP_STRIDE + (NSA_CMP_LEN - 1) <= t0 + qq
    nosel = jnp.zeros((nsel, tq), F32)
    imp = jnp.zeros((nc, tq), F32)
    o_c = []
    for h in range(NSA_HEADS):
        sT = jnp.where(vis, _dot(kc2, q2T_of(h, nosel)), NEG_INF)
        m = jnp.max(sT, axis=0, keepdims=True)
        e = jnp.where(vis, jnp.exp2(sT - m), 0.0)
        l = jnp.sum(e, axis=0, keepdims=True)
        pc = e / jnp.where(l > 0.0, l, 1.0)
        imp = imp + pc
        o_c.append(_dot(vcmpT_ref[0], pc.astype(BF16)))

    imp_sel = _dot(aT_ref[...], imp, precision=HIGHEST)
    jj, qs = _tile_iotas(nsel, tq)
    cur = (t0 + qs) // NSA_SEL_LEN
    causal = jj <= cur
    forced = (jj == 0) | (jj > cur - NSA_LOCAL)
    score = jnp.where(causal, jnp.where(forced, FORCE_SCORE, imp_sel), -FORCE_SCORE)
    score_ref[...] = score

    def rank_step(j2, rank):
        row = score_ref[pl.ds(j2, 1), :]
        ahead = (row > score) | ((row == score) & (j2 < jj))
        return rank + jnp.where(ahead, 1.0, 0.0)

    rank = lax.fori_loop(0, nsel, rank_step, jnp.zeros((nsel, tq), F32), unroll=8)
    selbias = jnp.where(rank < min(NSA_TOPN, nsel), 0.0, NEG_INF)

    kk, qk = _tile_iotas(TL, tq)

    def branch_scores(k_ref, tab_ref, oldest):
        def scores(j, h, t):
            k2 = jnp.concatenate([k_ref[0, pl.ds(j * TL, TL), :], tab_ref[pl.ds(j * TL, TL), :]], axis=1)
            sT = _dot(k2, q2_ref[h])
            if t is None:
                return jnp.where(kk <= qk, sT, NEG_INF)
            if oldest is not None:
                return jnp.where((t != oldest) | (kk > qk), sT, NEG_INF)
            return sT
        return scores

    for h in range(NSA_HEADS):
        q2_ref[h] = q2T_of(h, selbias)
    _flash_sweep(NSA_HEADS, i, i, lambda t: t, branch_scores(ks_ref, stab_ref, None),
                 lambda j, h: vT_ref[0, j, 0:HEAD_DIM, :], bufs, m_ref, l_ref, acc_ref)
    o_s = [acc_ref[h] / l_ref[h, 0:1, :] for h in range(NSA_HEADS)]

    for h in range(NSA_HEADS):
        q2_ref[h] = q2T_of(h, nosel)
    nprev = NSA_WINDOW // TL
    _flash_sweep(NSA_HEADS, i, jnp.minimum(i, nprev), lambda t: i - 1 - t,
                 branch_scores(kw_ref, wtab_ref, nprev - 1),
                 lambda j, h: vT_ref[0, j, HEAD_DIM:2 * HEAD_DIM, :], bufs, m_ref, l_ref, acc_ref)

    for h in range(NSA_HEADS):
        o_w = acc_ref[h] / l_ref[h, 0:1, :]
        g = gT_ref[0, 0]
        out = (g[h:h + 1, :] * o_c[h] + g[NSA_HEADS + h:NSA_HEADS + h + 1, :] * o_s[h]
               + g[2 * NSA_HEADS + h:2 * NSA_HEADS + h + 1, :] * o_w)
        o_ref[0, 0, h * HEAD_DIM:(h + 1) * HEAD_DIM, :] = out.astype(BF16)


def _nsa_tables(S, nc, nsel):
    pos = np.arange(S)
    onehot = jnp.asarray((pos[:, None] // NSA_SEL_LEN == np.arange(nsel)[None, :]).astype(np.float32))
    pfeat = jnp.stack(_split3(jnp.asarray(pos * LOG2E, F32)), axis=1).astype(F32)
    rest = jnp.zeros((S, LANES - nsel - 3), F32)
    stab = jnp.concatenate([onehot, pfeat, rest], axis=1).astype(BF16)
    wtab = jnp.concatenate([jnp.zeros_like(onehot), pfeat, rest], axis=1).astype(BF16)
    cend = np.arange(nc) * NSA_CMP_STRIDE + NSA_CMP_LEN - 1
    cfeat = jnp.stack(_split3(jnp.asarray(cend * LOG2E, F32)), axis=1).astype(F32)
    ctab = jnp.concatenate([jnp.zeros((nc, nsel), F32), cfeat, jnp.zeros((nc, LANES - nsel - 3), F32)], axis=1).astype(BF16)
    ratio = NSA_SEL_LEN // NSA_CMP_STRIDE
    span = NSA_CMP_LEN // NSA_CMP_STRIDE
    a = np.zeros((nsel, nc), np.float32)
    for j in range(nsel):
        for mm in range(ratio):
            for n2 in range(span):
                if j * ratio + mm + n2 < nc:
                    a[j, j * ratio + mm + n2] += 1.0
    return stab, wtab, ctab, jnp.asarray(a)


def _nsa_attention(p, kcmp, vcmpT):
    B, nT = p["nqT"].shape[:2]
    S = nT * TL
    nc = kcmp.shape[1]
    nsel = S // NSA_SEL_LEN
    stab, wtab, ctab, aT = _nsa_tables(S, nc, nsel)
    full = lambda a: pl.BlockSpec(a.shape, lambda b, i: (0,) * a.ndim)
    return pl.pallas_call(
        functools.partial(_nsa_body, nc=nc, nsel=nsel),
        grid=(B, nT),
        in_specs=[pl.BlockSpec((1, 1, NSA_W, TL), lambda b, i: (b, i, 0, 0)),
                  pl.BlockSpec((1, nc, LANES), lambda b, i: (b, 0, 0)),
                  pl.BlockSpec((1, HEAD_DIM, nc), lambda b, i: (b, 0, 0)),
                  full(ctab), full(aT),
                  pl.BlockSpec((1, S, LANES), lambda b, i: (b, 0, 0)),
                  pl.BlockSpec((1, S, LANES), lambda b, i: (b, 0, 0)),
                  pl.BlockSpec((1, nT, 2 * HEAD_DIM, TL), lambda b, i: (b, 0, 0, 0)),
                  pl.BlockSpec((1, 1, 16, TL), lambda b, i: (b, i, 0, 0)),
                  full(stab), full(wtab)],
        out_specs=pl.BlockSpec((1, 1, NSA_W, TL), lambda b, i: (b, i, 0, 0)),
        out_shape=jax.ShapeDtypeStruct((B, nT, NSA_W, TL), BF16),
        scratch_shapes=[pltpu.VMEM((nsel, TL), F32), pltpu.VMEM((NSA_HEADS, KAUG, TL), BF16)] + _flash_scratch(NSA_HEADS),
        compiler_params=pltpu.CompilerParams(dimension_semantics=("arbitrary",) * 2, vmem_limit_bytes=VMEM_LIMIT),
        name="nsa_attn",
    )(p["nqT"], kcmp, vcmpT, ctab, aT, p["ks"], p["kw"], p["nvT"], p["gT"], stab, wtab)


def _merge_body(x_ref, g_ref, ofT_ref, onT_ref, omT_ref, wfT_ref, wnT_ref, wmT_ref, wgT_ref, bg_ref, woT_ref, out_ref, outT_ref):
    x = x_ref[0]
    hb = _rms(x, g_ref[...]).astype(BF16)
    ys = (_dot(wfT_ref[...], ofT_ref[0, 0]), _dot(wnT_ref[...], onT_ref[0, 0]), _dot(wmT_ref[...], omT_ref[0, 0]))
    merged = jnp.zeros_like(ys[0])
    for r in range(N_BRANCHES):
        rows = slice(r * D_MODEL, (r + 1) * D_MODEL)
        gate = jax.nn.sigmoid(_nt_dot(wgT_ref[rows, :], hb) + bg_ref[rows, :])
        merged = merged + gate * ys[r]
    outT_ref[...] = _dot(woT_ref[...], merged.astype(BF16))
    out_ref[0] = x + outT_ref[...].T


def _merge(x, g, ofT, onT, omT, w_br_fox, w_br_nsa, w_br_moba, w_gate, b_gate, w_out):
    B, S, D = x.shape
    nT = S // TL
    wfT, wnT, wmT = w_br_fox.T.astype(BF16), w_br_nsa.T.astype(BF16), w_br_moba.T.astype(BF16)
    wgT, bg, woT = w_gate.T.astype(BF16), b_gate[:, None], w_out.T.astype(BF16)
    full = lambda a: pl.BlockSpec(a.shape, lambda b, i: (0,) * a.ndim)
    tiled = lambda n: pl.BlockSpec((1, 1, n, TL), lambda b, i: (b, i, 0, 0))
    return pl.pallas_call(
        _merge_body,
        grid=(B, nT),
        in_specs=[pl.BlockSpec((1, TL, D), lambda b, i: (b, i, 0)), full(g), tiled(FOX_W), tiled(NSA_W), tiled(MOBA_W),
                  full(wfT), full(wnT), full(wmT), full(wgT), full(bg), full(woT)],
        out_specs=pl.BlockSpec((1, TL, D), lambda b, i: (b, i, 0)),
        out_shape=jax.ShapeDtypeStruct((B, S, D), F32),
        scratch_shapes=[pltpu.VMEM((D, TL), F32)],
        compiler_params=pltpu.CompilerParams(dimension_semantics=("arbitrary",) * 2, vmem_limit_bytes=VMEM_LIMIT),
        name="merge",
    )(x, g, ofT, onT, omT, wfT, wnT, wmT, wgT, bg, woT)


def _route(lg):
    def softmax_rows(rows):
        m = functools.reduce(jnp.maximum, rows)
        es = [jnp.exp(r - m) for r in rows]
        tot = functools.reduce(lambda a, b: a + b, es)
        return [e / tot for e in es]

    def rank_of(vals, n):
        r = jnp.zeros_like(vals[0])
        for n2 in range(len(vals)):
            if n2 == n:
                continue
            ahead = (vals[n2] > vals[n]) | ((vals[n2] == vals[n]) & (n2 < n))
            r = r + jnp.where(ahead, 1.0, 0.0)
        return r

    gl = [lg[r:r + 1, :] for r in range(N_GROUPS)]
    gp = softmax_rows(gl)
    out = []
    for gi in range(N_GROUPS):
        gw = jnp.where(rank_of(gl, gi) < 1.0, gp[gi], 0.0)
        ep = softmax_rows([lg[8 + gi * EXPERTS_PER_GROUP + e: 9 + gi * EXPERTS_PER_GROUP + e, :]
                           for e in range(EXPERTS_PER_GROUP)])
        top = [jnp.where(rank_of(ep, e) < float(TOPK_IN_GROUP), ep[e], 0.0) for e in range(EXPERTS_PER_GROUP)]
        tot = functools.reduce(lambda a, b: a + b, top)
        out.extend([gw * (t / tot) for t in top])
    return jnp.concatenate(out, axis=0)


def _moe_body(x_ref, g_ref, wr_ref, br_ref, wg_ref, wu_ref, wd_ref, gf_ref, out_ref, hb_ref, comb_ref, *, final):
    e = pl.program_id(1)
    tm = x_ref.shape[0]

    @pl.when(e == 0)
    def _():
        x = x_ref[...]
        h = _rms(x, g_ref[...])
        hb_ref[...] = h.astype(BF16)
        combT = _route(_nt_dot(wr_ref[...], h, precision=HIGHEST) + br_ref[...])
        comb = jnp.concatenate([combT, jnp.zeros((LANES - N_EXPERTS, tm), F32)], axis=0).T
        for k in range(N_EXPERTS):
            comb_ref[k] = jnp.broadcast_to(comb[:, k:k + 1], (tm, LANES))
        out_ref[...] = x

    hb = hb_ref[...]
    a = _dot(hb, wg_ref[0])
    u = _dot(hb, wu_ref[0])
    c = comb_ref[e]
    hid = (a * jax.nn.sigmoid(a)) * u * jnp.concatenate([c] * (EXPERT_HIDDEN // LANES), axis=1)
    out_ref[...] += _dot(hid.astype(BF16), wd_ref[0])

    if final:
        @pl.when(e == N_EXPERTS - 1)
        def _():
            out_ref[...] = _rms(out_ref[...], gf_ref[...])


def _moe(x, g, w_route_grp, b_route_grp, w_route_exp, b_route_exp, w_exp_gate, w_exp_up, w_exp_down, g_final, final, tm=1024):
    B, S, D = x.shape
    T = B * S
    xt = x.reshape(T, D)
    wr = jnp.zeros((32, D), F32).at[0:N_GROUPS].set(w_route_grp.T).at[8:8 + N_EXPERTS].set(w_route_exp.T)
    br = jnp.zeros((32, 1), F32).at[0:N_GROUPS, 0].set(b_route_grp).at[8:8 + N_EXPERTS, 0].set(b_route_exp)
    wg, wu, wd = w_exp_gate.astype(BF16), w_exp_up.astype(BF16), w_exp_down.astype(BF16)
    full = lambda a: pl.BlockSpec(a.shape, lambda t, e: (0,) * a.ndim)
    out = pl.pallas_call(
        functools.partial(_moe_body, final=final),
        grid=(T // tm, N_EXPERTS),
        in_specs=[pl.BlockSpec((tm, D), lambda t, e: (t, 0)), full(g), full(wr), full(br),
                  pl.BlockSpec((1, D, EXPERT_HIDDEN), lambda t, e: (e, 0, 0)),
                  pl.BlockSpec((1, D, EXPERT_HIDDEN), lambda t, e: (e, 0, 0)),
                  pl.BlockSpec((1, EXPERT_HIDDEN, D), lambda t, e: (e, 0, 0)),
                  full(g_final)],
        out_specs=pl.BlockSpec((tm, D), lambda t, e: (t, 0)),
        out_shape=jax.ShapeDtypeStruct((T, D), F32),
        scratch_shapes=[pltpu.VMEM((tm, D), BF16), pltpu.VMEM((N_EXPERTS, tm, LANES), F32)],
        compiler_params=pltpu.CompilerParams(dimension_semantics=("arbitrary",) * 2, vmem_limit_bytes=VMEM_LIMIT),
        name="moe",
    )(xt, g, wr, br, wg, wu, wd, g_final)
    return out.reshape(B, S, D)


def _fox_placement():
    e = np.zeros((FOX_HEADS, 3, LANES, LANES), np.float32)
    for h in range(FOX_HEADS):
        for j in range(3):
            e[h, j, h, j] = 1.0
    return jnp.asarray(e, BF16)


def kernel(x, norm_mix_g, norm_ffn_g, final_norm_g, w_in, b_in, cmp_w1, cmp_b1, cmp_w2, cmp_pos, w_br_fox, w_br_nsa, w_br_moba, w_out, w_route_grp, b_route_grp, w_route_exp, b_route_exp, w_exp_gate, w_exp_up, w_exp_down):
    depth = w_in.shape[0]
    e_fox = _fox_placement()
    gf = final_norm_g[None, :]
    for l in range(depth):
        g_mix = norm_mix_g[l][None, :]
        ws, bs, wt, bt = _proj_weights(w_in[l], b_in[l])
        p = _project(x, g_mix, ws, bs, wt, bt)
        ofT = _fox_attention(p, e_fox)
        kcmp, vcmpT = _nsa_compress(p, cmp_w1[l], cmp_b1[l], cmp_w2[l], cmp_pos[l])
        onT = _nsa_attention(p, kcmp, vcmpT)
        omT = _moba_attention(p)
        x = _merge(x, g_mix, ofT, onT, omT, w_br_fox[l], w_br_nsa[l], w_br_moba[l],
                   w_in[l][:, _O_ML:_O_END], b_in[l][_O_ML:_O_END], w_out[l])
        x = _moe(x, norm_ffn_g[l][None, :], w_route_grp[l], b_route_grp[l], w_route_exp[l], b_route_exp[l],
                 w_exp_gate[l], w_exp_up[l], w_exp_down[l], gf, final=(l == depth - 1))
    return x
```

```python
import functools

import numpy as np
import jax
import jax.numpy as jnp
from jax import lax
from jax.experimental import pallas as pl
from jax.experimental.pallas import tpu as pltpu

F32 = jnp.float32
BF16 = jnp.bfloat16
HIGHEST = lax.Precision.HIGHEST

D_MODEL = 1024
HEAD_DIM = 64
FOX_HEADS = 6
NSA_HEADS = 4
MOBA_HEADS = 6
FOX_W = FOX_HEADS * HEAD_DIM
NSA_W = NSA_HEADS * HEAD_DIM
MOBA_W = MOBA_HEADS * HEAD_DIM
N_BRANCHES = 3

NSA_CMP_LEN = 32
NSA_CMP_STRIDE = 16
NSA_CMP_HIDDEN = 256
NSA_SEL_LEN = 64
NSA_TOPN = 16
NSA_LOCAL = 2
NSA_WINDOW = 512
MOBA_BLOCK = 256
MOBA_TOPK = 3

N_GROUPS = 4
EXPERTS_PER_GROUP = 4
N_EXPERTS = N_GROUPS * EXPERTS_PER_GROUP
TOPK_IN_GROUP = 2
EXPERT_HIDDEN = 256

RMS_EPS = 1e-6
NEG_INF = -1e30
FORCE_SCORE = 1e9

LANES = 128
TL = 256
KAUG = 2 * LANES
VMEM_LIMIT = 56 * 1024 * 1024

_O_FQ = 0
_O_FK = _O_FQ + FOX_W
_O_FV = _O_FK + FOX_W
_O_FF = _O_FV + FOX_W
_O_NQ = _O_FF + FOX_HEADS
_O_KC = _O_NQ + NSA_W
_O_VC = _O_KC + HEAD_DIM
_O_KS = _O_VC + HEAD_DIM
_O_VS = _O_KS + HEAD_DIM
_O_KW = _O_VS + HEAD_DIM
_O_VW = _O_KW + HEAD_DIM
_O_NG = _O_VW + HEAD_DIM
_O_MQ = _O_NG + N_BRANCHES * NSA_HEADS
_O_MK = _O_MQ + MOBA_W
_O_MV = _O_MK + MOBA_W
_O_ML = _O_MV + MOBA_W
_O_END = _O_ML + N_BRANCHES * D_MODEL

_S_FK = 0
_S_MK = _S_FK + FOX_HEADS * LANES
_S_KS = _S_MK + MOBA_HEADS * LANES
_S_KW = _S_KS + LANES
_S_KC = _S_KW + LANES
_S_VC = _S_KC + LANES
_S_FF = _S_VC + LANES
_S_END = _S_FF + LANES
_T_FQ = 0
_T_FV = _T_FQ + FOX_W
_T_MQ = _T_FV + FOX_W
_T_MV = _T_MQ + MOBA_W
_T_NQ = _T_MV + MOBA_W
_T_NV = _T_NQ + NSA_W
_T_NG = _T_NV + 2 * HEAD_DIM
_T_END = _T_NG + 16

LOG2E = float(np.log2(np.e))
Q_SCALE = HEAD_DIM ** -0.5 * LOG2E


def _alibi_slopes(n):
    return 2.0 ** (-8.0 * np.arange(1, n + 1) / n)


def _split3(v):
    hi = v.astype(BF16)
    r1 = v - hi.astype(F32)
    lo = r1.astype(BF16)
    r2 = r1 - lo.astype(F32)
    return hi, lo, r2.astype(BF16)


def _nt_dot(a, b, precision=None):
    return lax.dot_general(a, b, (((1,), (1,)), ((), ())), preferred_element_type=F32, precision=precision)


def _dot(a, b, precision=None):
    return jnp.dot(a, b, preferred_element_type=F32, precision=precision)


def _rms(x, g):
    return x * lax.rsqrt(jnp.mean(x * x, axis=-1, keepdims=True) + RMS_EPS) * g


def _proj_body(x_ref, g_ref, ws_ref, bs_ref, wt_ref, bt_ref,
               fk_ref, mk_ref, ks_ref, kw_ref, kc_ref, vc_ref, negc_ref, kmean_ref,
               fqT_ref, fvT_ref, mqT_ref, mvT_ref, nqT_ref, nvT_ref, gT_ref, carry_ref, *, tm):
    i = pl.program_id(1)
    hb = _rms(x_ref[0], g_ref[...]).astype(BF16)

    def sdot(a, b):
        return _dot(hb, ws_ref[:, a:b]) + bs_ref[:, a:b]

    def tdot(a, b):
        return _nt_dot(wt_ref[a:b, :], hb) + bt_ref[a:b, :]

    def put_t(ref, val):
        for c in range(tm // TL):
            ref[0, c] = val[:, c * TL:(c + 1) * TL]

    fk_ref[0] = sdot(_S_FK, _S_MK).astype(BF16)
    mk = sdot(_S_MK, _S_KS)
    mk_ref[0] = mk.astype(BF16)
    nblk = tm // MOBA_BLOCK
    means = [jnp.mean(mk[c * MOBA_BLOCK:(c + 1) * MOBA_BLOCK, :], axis=0, keepdims=True) for c in range(nblk)]
    kmean_ref[0, 0] = jnp.concatenate(means + [jnp.zeros((8 - nblk, mk.shape[1]), F32)], axis=0)
    ks_ref[0] = sdot(_S_KS, _S_KW).astype(BF16)
    kw_ref[0] = sdot(_S_KW, _S_KC).astype(BF16)
    kc_ref[0] = sdot(_S_KC, _S_VC)
    vc_ref[0] = sdot(_S_VC, _S_FF)

    @pl.when(i == 0)
    def _():
        carry_ref[...] = jnp.zeros_like(carry_ref)

    fl = sdot(_S_FF, _S_END)
    ls = jnp.minimum(fl, 0.0) - jnp.log1p(jnp.exp(-jnp.abs(fl)))
    ls = jnp.where(lax.broadcasted_iota(jnp.int32, ls.shape, 1) < FOX_HEADS, ls, 0.0)
    tri = (lax.broadcasted_iota(jnp.int32, (tm, tm), 1) <= lax.broadcasted_iota(jnp.int32, (tm, tm), 0)).astype(F32)
    cs = _dot(tri, ls, precision=HIGHEST) + carry_ref[0:1, :]
    negc_ref[0] = -cs
    carry_ref[0:1, :] = cs[tm - 1:tm, :]

    put_t(fqT_ref, tdot(_T_FQ, _T_FV).astype(BF16))
    put_t(fvT_ref, tdot(_T_FV, _T_MQ).astype(BF16))
    put_t(mqT_ref, tdot(_T_MQ, _T_MV).astype(BF16))
    put_t(mvT_ref, tdot(_T_MV, _T_NQ).astype(BF16))
    put_t(nqT_ref, tdot(_T_NQ, _T_NV).astype(BF16))
    put_t(nvT_ref, tdot(_T_NV, _T_NG).astype(BF16))
    put_t(gT_ref, jax.nn.sigmoid(tdot(_T_NG, _T_END)))


def _proj_weights(w, b):
    def pad_heads(cols, nh):
        m = cols.reshape(cols.shape[0], nh, HEAD_DIM)
        return jnp.pad(m, ((0, 0), (0, 0), (0, LANES - HEAD_DIM))).reshape(cols.shape[0], nh * LANES)

    def pad_to(cols, n):
        return jnp.pad(cols, ((0, 0), (0, n - cols.shape[1])))

    wb = jnp.concatenate([w, b[None, :]], axis=0)
    sl = lambda o, n: wb[:, o:o + n]
    std = jnp.concatenate([
        pad_heads(sl(_O_FK, FOX_W), FOX_HEADS), pad_heads(sl(_O_MK, MOBA_W), MOBA_HEADS),
        pad_to(sl(_O_KS, HEAD_DIM), LANES), pad_to(sl(_O_KW, HEAD_DIM), LANES),
        pad_to(sl(_O_KC, HEAD_DIM), LANES), pad_to(sl(_O_VC, HEAD_DIM), LANES),
        pad_to(sl(_O_FF, FOX_HEADS), LANES)], axis=1)
    scale = Q_SCALE
    tr = jnp.concatenate([
        sl(_O_FQ, FOX_W) * scale, sl(_O_FV, FOX_W), sl(_O_MQ, MOBA_W) * scale, sl(_O_MV, MOBA_W),
        sl(_O_NQ, NSA_W) * scale, sl(_O_VS, HEAD_DIM), sl(_O_VW, HEAD_DIM),
        pad_to(sl(_O_NG, N_BRANCHES * NSA_HEADS), 16)], axis=1)
    ws, bs = std[:-1].astype(BF16), std[-1:].astype(F32)
    wt, bt = tr[:-1].T.astype(BF16), tr[-1:].T.astype(F32)
    return ws, bs, wt, bt


def _project(x, g, ws, bs, wt, bt, tm=512):
    B, S, D = x.shape
    nT = S // TL
    nt = tm // TL
    grid = (B, S // tm)
    row = lambda n, dt: jax.ShapeDtypeStruct((B, S, n), dt)
    tiled = lambda n, dt: jax.ShapeDtypeStruct((B, nT, n, TL), dt)
    row_spec = lambda n: pl.BlockSpec((1, tm, n), lambda b, i: (b, i, 0))
    tiled_spec = lambda n: pl.BlockSpec((1, nt, n, TL), lambda b, i: (b, i, 0, 0))
    full = lambda a: pl.BlockSpec(a.shape, lambda b, i: (0,) * a.ndim)
    out_shape = [row(FOX_HEADS * LANES, BF16), row(MOBA_HEADS * LANES, BF16), row(LANES, BF16), row(LANES, BF16),
                 row(LANES, F32), row(LANES, F32), row(LANES, F32),
                 jax.ShapeDtypeStruct((B, S // tm, 8, MOBA_HEADS * LANES), F32),
                 tiled(FOX_W, BF16), tiled(FOX_W, BF16), tiled(MOBA_W, BF16), tiled(MOBA_W, BF16),
                 tiled(NSA_W, BF16), tiled(2 * HEAD_DIM, BF16), tiled(16, F32)]
    out_specs = [row_spec(FOX_HEADS * LANES), row_spec(MOBA_HEADS * LANES), row_spec(LANES), row_spec(LANES),
                 row_spec(LANES), row_spec(LANES), row_spec(LANES),
                 pl.BlockSpec((1, 1, 8, MOBA_HEADS * LANES), lambda b, i: (b, i, 0, 0)),
                 tiled_spec(FOX_W), tiled_spec(FOX_W), tiled_spec(MOBA_W), tiled_spec(MOBA_W),
                 tiled_spec(NSA_W), tiled_spec(2 * HEAD_DIM), tiled_spec(16)]
    outs = pl.pallas_call(
        functools.partial(_proj_body, tm=tm),
        grid=grid,
        in_specs=[pl.BlockSpec((1, tm, D), lambda b, i: (b, i, 0)), full(g), full(ws), full(bs), full(wt), full(bt)],
        out_specs=out_specs,
        out_shape=out_shape,
        scratch_shapes=[pltpu.VMEM((8, LANES), F32)],
        compiler_params=pltpu.CompilerParams(dimension_semantics=("arbitrary", "arbitrary"),
                                             vmem_limit_bytes=VMEM_LIMIT),
        name="proj",
    )(x, g, ws, bs, wt, bt)
    (fk, mk, ks, kw, kc, vc, negc, kmean8, fqT, fvT, mqT, mvT, nqT, nvT, gT) = outs
    nblk = tm // MOBA_BLOCK
    kmean = kmean8[:, :, :nblk, :].reshape(B, (S // tm) * nblk, MOBA_HEADS * LANES)
    return dict(fk=fk, mk=mk, ks=ks, kw=kw, kc=kc, vc=vc, negc=negc, kmean=kmean,
                fqT=fqT, fvT=fvT, mqT=mqT, mvT=mvT, nqT=nqT, nvT=nvT, gT=gT)


def _flash_sweep(n_heads, diag_tile, n_rest, rest_tile, scores, value_tile, bufs, m_ref, l_ref, acc_ref):
    half = n_heads // 2
    groups = (tuple(range(half)), tuple(range(half, n_heads)))
    for h in range(n_heads):
        m_ref[h] = jnp.full(m_ref.shape[1:], NEG_INF, F32)
        l_ref[h] = jnp.zeros(l_ref.shape[1:], F32)
        acc_ref[h] = jnp.zeros(acc_ref.shape[1:], F32)

    def produce(j, grp, t):
        mts = []
        for h in grp:
            sT = scores(j, h, t)
            bufs[h][...] = sT
            mts.append(jnp.max(sT, axis=0, keepdims=True))
        return tuple(mts)

    def consume(j, grp, mts):
        for h, mt in zip(grp, mts):
            m = m_ref[h, 0:1, :]
            m_new = jnp.maximum(m, mt)
            p = jnp.exp2(bufs[h][...] - m_new)
            alpha = jnp.exp2(m - m_new)
            m_ref[h, 0:1, :] = m_new
            l_ref[h, 0:1, :] = alpha * l_ref[h, 0:1, :] + jnp.sum(p, axis=0, keepdims=True)
            acc_ref[h] = alpha * acc_ref[h] + _dot(value_tile(j, h), p.astype(BF16))

    def tile_steps(j, t, prev):
        mts0 = produce(j, groups[0], t)
        if prev is not None:
            consume(prev[0], groups[1], prev[1])
        mts1 = produce(j, groups[1], t)
        consume(j, groups[0], mts0)
        return mts1

    def step(t, mts):
        prev_tile = jnp.where(t == 0, diag_tile, rest_tile(t - 1))
        return tile_steps(rest_tile(t), t, (prev_tile, mts))

    mts = lax.fori_loop(0, n_rest // 2, lambda u, mts: step(2 * u + 1, step(2 * u, mts)),
                        tile_steps(diag_tile, None, None))
    mts = lax.cond(n_rest % 2 == 1, lambda mts: step(n_rest - 1, mts), lambda mts: mts, mts)
    consume(jnp.where(n_rest == 0, diag_tile, rest_tile(n_rest - 1)), groups[1], mts)


def _tile_iotas(tk, tq):
    kk = lax.broadcasted_iota(jnp.int32, (tk, tq), 0)
    qq = lax.broadcasted_iota(jnp.int32, (tk, tq), 1)
    return kk, qq


def _feature_rows(nrows, tq, value):
    r = lax.broadcasted_iota(jnp.int32, (16, tq), 0)
    return jnp.where(r < nrows, value, 0.0).astype(F32)


def _causal_keep(tq):
    kk, qq = _tile_iotas(TL, tq)
    return kk <= qq


def _flash_scratch(n_heads):
    return ([pltpu.VMEM((TL, TL), F32) for _ in range(n_heads)]
            + [pltpu.VMEM((n_heads, 8, TL), F32), pltpu.VMEM((n_heads, 8, TL), F32),
               pltpu.VMEM((n_heads, HEAD_DIM, TL), F32)])


def _fox_body(qT_ref, k_ref, vT_ref, negc_ref, e_ref, o_ref, kfeat_ref, q2_ref, *flash, S, hps):
    bufs, (m_ref, l_ref, acc_ref) = flash[:hps], flash[hps:]
    i = pl.program_id(2)
    tq = TL

    @pl.when(i == 0)
    def _():
        chunk = 512
        for c in range(S // chunk):
            hi, lo, lo2 = _split3(negc_ref[0, c * chunk:(c + 1) * chunk, :] * LOG2E)
            for h in range(hps):
                feat = _dot(hi, e_ref[h, 0]) + _dot(lo, e_ref[h, 1]) + _dot(lo2, e_ref[h, 2])
                kfeat_ref[h, c * chunk:(c + 1) * chunk, :] = feat.astype(BF16)

    feat_rows = jnp.concatenate([_feature_rows(3, tq, 1.0), jnp.zeros((LANES - 16, tq), F32)], axis=0).astype(BF16)
    for h in range(hps):
        q2_ref[h] = jnp.concatenate([qT_ref[0, 0, h * HEAD_DIM:(h + 1) * HEAD_DIM, :],
                                     jnp.zeros((HEAD_DIM, tq), BF16), feat_rows], axis=0)

    def scores(j, h, t):
        k2 = jnp.concatenate([k_ref[0, pl.ds(j * TL, TL), h * LANES:(h + 1) * LANES],
                              kfeat_ref[h, pl.ds(j * TL, TL), :]], axis=1)
        sT = _dot(k2, q2_ref[h])
        return jnp.where(_causal_keep(tq), sT, NEG_INF) if t is None else sT

    _flash_sweep(hps, i, i, lambda t: t, scores, lambda j, h: vT_ref[0, j, h * HEAD_DIM:(h + 1) * HEAD_DIM, :],
                 bufs, m_ref, l_ref, acc_ref)
    for h in range(hps):
        o_ref[0, 0, h * HEAD_DIM:(h + 1) * HEAD_DIM, :] = (acc_ref[h] / l_ref[h, 0:1, :]).astype(BF16)


def _fox_attention(p, e_fox, hps=FOX_HEADS):
    B, nT = p["fqT"].shape[:2]
    S = nT * TL
    return pl.pallas_call(
        functools.partial(_fox_body, S=S, hps=hps),
        grid=(B, FOX_HEADS // hps, nT),
        in_specs=[pl.BlockSpec((1, 1, hps * HEAD_DIM, TL), lambda b, h, i: (b, i, h, 0)),
                  pl.BlockSpec((1, S, hps * LANES), lambda b, h, i: (b, 0, h)),
                  pl.BlockSpec((1, nT, hps * HEAD_DIM, TL), lambda b, h, i: (b, 0, h, 0)),
                  pl.BlockSpec((1, S, LANES), lambda b, h, i: (b, 0, 0)),
                  pl.BlockSpec((hps, 3, LANES, LANES), lambda b, h, i: (h, 0, 0, 0))],
        out_specs=pl.BlockSpec((1, 1, hps * HEAD_DIM, TL), lambda b, h, i: (b, i, h, 0)),
        out_shape=jax.ShapeDtypeStruct((B, nT, FOX_W, TL), BF16),
        scratch_shapes=[pltpu.VMEM((hps, S, LANES), BF16), pltpu.VMEM((hps, KAUG, TL), BF16)] + _flash_scratch(hps),
        compiler_params=pltpu.CompilerParams(dimension_semantics=("arbitrary",) * 3, vmem_limit_bytes=VMEM_LIMIT),
        name="fox_attn",
    )(p["fqT"], p["fk"], p["fvT"], p["negc"], e_fox)


def _rank_rows(score, nrows):
    ridx = lax.broadcasted_iota(jnp.int32, score.shape, 0)
    rank = jnp.zeros(score.shape, F32)
    for n2 in range(nrows):
        row = score[n2:n2 + 1, :]
        ahead = (row > score) | ((row == score) & (n2 < ridx))
        rank = rank + jnp.where(ahead, 1.0, 0.0)
    return rank


def _moba_body(qT_ref, k_ref, vT_ref, kmean_ref, ptab_ref, o_ref, q2_ref, *flash, nbp, hps):
    bufs, (m_ref, l_ref, acc_ref) = flash[:hps], flash[hps:]
    i = pl.program_id(2)
    tq = TL
    zpad = jnp.zeros((HEAD_DIM, tq), BF16)
    blk = lax.broadcasted_iota(jnp.int32, (nbp, tq), 0)
    for h in range(hps):
        qT = qT_ref[0, 0, h * HEAD_DIM:(h + 1) * HEAD_DIM, :]
        gate = _dot(kmean_ref[0, :, h * LANES:(h + 1) * LANES].astype(BF16),
                    jnp.concatenate([qT, zpad], axis=0)) * (1.0 / Q_SCALE)
        gate = jnp.where(blk < i, gate, -FORCE_SCORE)
        sel = (_rank_rows(gate, nbp) < MOBA_TOPK) & (blk < i)
        selbias = jnp.where(sel | (blk == i), 0.0, NEG_INF)
        feat_rows = jnp.concatenate([selbias, _feature_rows(3, tq, 1.0), jnp.zeros((LANES - nbp - 16, tq), F32)], axis=0)
        q2_ref[h] = jnp.concatenate([qT, zpad, feat_rows.astype(BF16)], axis=0)

    def scores(j, h, t):
        k2 = jnp.concatenate([k_ref[0, pl.ds(j * TL, TL), h * LANES:(h + 1) * LANES],
                              ptab_ref[h, pl.ds(j * TL, TL), :]], axis=1)
        sT = _dot(k2, q2_ref[h])
        return jnp.where(_causal_keep(tq), sT, NEG_INF) if t is None else sT

    _flash_sweep(hps, i, i, lambda t: t, scores, lambda j, h: vT_ref[0, j, h * HEAD_DIM:(h + 1) * HEAD_DIM, :],
                 bufs, m_ref, l_ref, acc_ref)
    for h in range(hps):
        o_ref[0, 0, h * HEAD_DIM:(h + 1) * HEAD_DIM, :] = (acc_ref[h] / l_ref[h, 0:1, :]).astype(BF16)


def _moba_tables(S, nbp):
    pos = np.arange(S)
    onehot = (pos[:, None] // MOBA_BLOCK == np.arange(nbp)[None, :]).astype(np.float32)
    tabs = []
    for slope in _alibi_slopes(MOBA_HEADS):
        hi, lo, lo2 = _split3(jnp.asarray(slope * LOG2E * pos, F32))
        feat = jnp.stack([hi, lo, lo2], axis=1).astype(F32)
        tabs.append(jnp.concatenate([jnp.asarray(onehot), feat, jnp.zeros((S, LANES - nbp - 3), F32)], axis=1))
    return jnp.stack(tabs).astype(BF16)


def _moba_attention(p, hps=MOBA_HEADS):
    B, nT = p["mqT"].shape[:2]
    S = nT * TL
    nbp = -(-nT // 16) * 16
    kmean = jnp.pad(p["kmean"], ((0, 0), (0, nbp - nT), (0, 0)))
    ptab = _moba_tables(S, nbp)
    return pl.pallas_call(
        functools.partial(_moba_body, nbp=nbp, hps=hps),
        grid=(B, MOBA_HEADS // hps, nT),
        in_specs=[pl.BlockSpec((1, 1, hps * HEAD_DIM, TL), lambda b, h, i: (b, i, h, 0)),
                  pl.BlockSpec((1, S, hps * LANES), lambda b, h, i: (b, 0, h)),
                  pl.BlockSpec((1, nT, hps * HEAD_DIM, TL), lambda b, h, i: (b, 0, h, 0)),
                  pl.BlockSpec((1, nbp, hps * LANES), lambda b, h, i: (b, 0, h)),
                  pl.BlockSpec((hps, S, LANES), lambda b, h, i: (h, 0, 0))],
        out_specs=pl.BlockSpec((1, 1, hps * HEAD_DIM, TL), lambda b, h, i: (b, i, h, 0)),
        out_shape=jax.ShapeDtypeStruct((B, nT, MOBA_W, TL), BF16),
        scratch_shapes=[pltpu.VMEM((hps, KAUG, TL), BF16)] + _flash_scratch(hps),
        compiler_params=pltpu.CompilerParams(dimension_semantics=("arbitrary",) * 3, vmem_limit_bytes=VMEM_LIMIT),
        name="moba_attn",
    )(p["mqT"], p["mk"], p["mvT"], kmean, ptab)


def _cmp_body(kc_ref, vc_ref, w1_ref, b1_ref, w2k_ref, w2vT_ref, pos_ref, kcmp_ref, vcmpT_ref, pad_ref, *, S, nc):
    tail = pad_ref.shape[0] - S

    def hidden(src_ref, which):
        pad_ref[0:S, :] = src_ref[0]
        pad_ref[S:S + tail, :] = jnp.zeros((tail, LANES), F32)
        acc = jnp.zeros((nc, NSA_CMP_HIDDEN), F32)
        for l in range(NSA_CMP_LEN):
            rows = pad_ref[pl.ds(l, nc, stride=NSA_CMP_STRIDE), :] + pos_ref[which, l:l + 1, :]
            acc = acc + _dot(rows.astype(BF16), w1_ref[which, l])
        return jax.nn.gelu(acc + b1_ref[which]).astype(BF16)

    kcmp_ref[0] = _dot(hidden(kc_ref, 0), w2k_ref[...]).astype(BF16)
    vcmpT_ref[0] = _nt_dot(w2vT_ref[...], hidden(vc_ref, 1)).astype(BF16)


def _nsa_compress(p, cmp_w1, cmp_b1, cmp_w2, cmp_pos):
    B, S, _ = p["kc"].shape
    nc = S // NSA_CMP_STRIDE
    w1 = jnp.pad(cmp_w1, ((0, 0), (0, 0), (0, LANES - HEAD_DIM), (0, 0))).astype(BF16)
    pos = jnp.pad(cmp_pos, ((0, 0), (0, 0), (0, LANES - HEAD_DIM)))
    b1 = cmp_b1[:, None, :]
    w2k = jnp.pad(cmp_w2[0], ((0, 0), (0, LANES - HEAD_DIM))).astype(BF16)
    w2vT = cmp_w2[1].T.astype(BF16)
    full = lambda a: pl.BlockSpec(a.shape, lambda b: (0,) * a.ndim)
    return pl.pallas_call(
        functools.partial(_cmp_body, S=S, nc=nc),
        grid=(B,),
        in_specs=[pl.BlockSpec((1, S, LANES), lambda b: (b, 0, 0)), pl.BlockSpec((1, S, LANES), lambda b: (b, 0, 0)),
                  full(w1), full(b1), full(w2k), full(w2vT), full(pos)],
        out_specs=[pl.BlockSpec((1, nc, LANES), lambda b: (b, 0, 0)), pl.BlockSpec((1, HEAD_DIM, nc), lambda b: (b, 0, 0))],
        out_shape=[jax.ShapeDtypeStruct((B, nc, LANES), BF16), jax.ShapeDtypeStruct((B, HEAD_DIM, nc), BF16)],
        scratch_shapes=[pltpu.VMEM((S + NSA_CMP_LEN, LANES), F32)],
        compiler_params=pltpu.CompilerParams(dimension_semantics=("arbitrary",), vmem_limit_bytes=VMEM_LIMIT),
        name="nsa_compress",
    )(p["kc"], p["vc"], w1, b1, w2k, w2vT, pos)


def _nsa_body(qT_ref, kcmp_ref, vcmpT_ref, ctab_ref, aT_ref, ks_ref, kw_ref, vT_ref, gT_ref, stab_ref, wtab_ref,
              o_ref, score_ref, q2_ref, *flash, nc, nsel):
    bufs, (m_ref, l_ref, acc_ref) = flash[:NSA_HEADS], flash[NSA_HEADS:]
    i = pl.program_id(1)
    tq = TL
    t0 = i * tq
    slopes = _alibi_slopes(NSA_HEADS)
    zpad = jnp.zeros((HEAD_DIM, tq), BF16)

    def q2T_of(h, selbias):
        rows = jnp.concatenate([selbias, _feature_rows(3, tq, float(slopes[h])),
                                jnp.zeros((LANES - nsel - 16, tq), F32)], axis=0)
        return jnp.concatenate([qT_ref[0, 0, h * HEAD_DIM:(h + 1) * HEAD_DIM, :], zpad, rows.astype(BF16)], axis=0)

    kc2 = jnp.concatenate([kcmp_ref[0], ctab_ref[...]], axis=1)
    nn, qq = _tile_iotas(nc, tq)
    vis = nn * NSA_CMP_STRIDE + (NSA_CMP_LEN - 1) <= t0 + qq
    nosel = jnp.zeros((nsel, tq), F32)
    m_c = []
    for h in range(NSA_HEADS):
        sT = jnp.where(vis, _dot(kc2, q2T_of(h, nosel)), NEG_INF)
        bufs[h][0:nc, :] = sT
        m_c.append(jnp.max(sT, axis=0, keepdims=True))
    imp = jnp.zeros((nc, tq), F32)
    o_c = []
    for h in range(NSA_HEADS):
        e = jnp.where(vis, jnp.exp2(bufs[h][0:nc, :] - m_c[h]), 0.0)
        l = jnp.sum(e, axis=0, keepdims=True)
        pc = e * (1.0 / jnp.where(l > 0.0, l, 1.0))
        imp = imp + pc
        o_c.append(_dot(vcmpT_ref[0], pc.astype(BF16)))

    imp_sel = _dot(aT_ref[...], imp, precision=HIGHEST)
    jj, qs = _tile_iotas(nsel, tq)
    cur = (t0 + qs) // NSA_SEL_LEN
    causal = jj <= cur
    forced = (jj == 0) | (jj > cur - NSA_LOCAL)
    score = jnp.where(causal, jnp.where(forced, FORCE_SCORE, imp_sel), -FORCE_SCORE)
    score_ref[...] = score

    def rank_step(j2, rank):
        row = score_ref[pl.ds(j2, 1), :]
        ahead = (row > score) | ((row == score) & (j2 < jj))
        return rank + jnp.where(ahead, 1.0, 0.0)

    rank = lax.fori_loop(0, nsel, rank_step, jnp.zeros((nsel, tq), F32), unroll=8)
    selbias = jnp.where(rank < min(NSA_TOPN, nsel), 0.0, NEG_INF)

    kk, qk = _tile_iotas(TL, tq)

    def branch_scores(k_ref, tab_ref, oldest):
        def scores(j, h, t):
            k2 = jnp.concatenate([k_ref[0, pl.ds(j * TL, TL), :], tab_ref[pl.ds(j * TL, TL), :]], axis=1)
            sT = _dot(k2, q2_ref[h])
            if t is None:
                return jnp.where(kk <= qk, sT, NEG_INF)
            if oldest is not None:
                return jnp.where((t != oldest) | (kk > qk), sT, NEG_INF)
            return sT
        return scores

    for h in range(NSA_HEADS):
        q2_ref[h] = q2T_of(h, selbias)
    _flash_sweep(NSA_HEADS, i, i, lambda t: t, branch_scores(ks_ref, stab_ref, None),
                 lambda j, h: vT_ref[0, j, 0:HEAD_DIM, :], bufs, m_ref, l_ref, acc_ref)
    o_s = [acc_ref[h] / l_ref[h, 0:1, :] for h in range(NSA_HEADS)]

    for h in range(NSA_HEADS):
        q2_ref[h] = q2T_of(h, nosel)
    nprev = NSA_WINDOW // TL
    _flash_sweep(NSA_HEADS, i, jnp.minimum(i, nprev), lambda t: i - 1 - t,
                 branch_scores(kw_ref, wtab_ref, nprev - 1),
                 lambda j, h: vT_ref[0, j, HEAD_DIM:2 * HEAD_DIM, :], bufs, m_ref, l_ref, acc_ref)

    for h in range(NSA_HEADS):
        o_w = acc_ref[h] / l_ref[h, 0:1, :]
        g = gT_ref[0, 0]
        out = (g[h:h + 1, :] * o_c[h] + g[NSA_HEADS + h:NSA_HEADS + h + 1, :] * o_s[h]
               + g[2 * NSA_HEADS + h:2 * NSA_HEADS + h + 1, :] * o_w)
        o_ref[0, 0, h * HEAD_DIM:(h + 1) * HEAD_DIM, :] = out.astype(BF16)


def _nsa_tables(S, nc, nsel):
    pos = np.arange(S)
    onehot = jnp.asarray((pos[:, None] // NSA_SEL_LEN == np.arange(nsel)[None, :]).astype(np.float32))
    pfeat = jnp.stack(_split3(jnp.asarray(pos * LOG2E, F32)), axis=1).astype(F32)
    rest = jnp.zeros((S, LANES - nsel - 3), F32)
    stab = jnp.concatenate([onehot, pfeat, rest], axis=1).astype(BF16)
    wtab = jnp.concatenate([jnp.zeros_like(onehot), pfeat, rest], axis=1).astype(BF16)
    cend = np.arange(nc) * NSA_CMP_STRIDE + NSA_CMP_LEN - 1
    cfeat = jnp.stack(_split3(jnp.asarray(cend * LOG2E, F32)), axis=1).astype(F32)
    ctab = jnp.concatenate([jnp.zeros((nc, nsel), F32), cfeat, jnp.zeros((nc, LANES - nsel - 3), F32)], axis=1).astype(BF16)
    ratio = NSA_SEL_LEN // NSA_CMP_STRIDE
    span = NSA_CMP_LEN // NSA_CMP_STRIDE
    a = np.zeros((nsel, nc), np.float32)
    for j in range(nsel):
        for mm in range(ratio):
            for n2 in range(span):
                if j * ratio + mm + n2 < nc:
                    a[j, j * ratio + mm + n2] += 1.0
    return stab, wtab, ctab, jnp.asarray(a)


def _nsa_attention(p, kcmp, vcmpT):
    B, nT = p["nqT"].shape[:2]
    S = nT * TL
    nc = kcmp.shape[1]
    nsel = S // NSA_SEL_LEN
    stab, wtab, ctab, aT = _nsa_tables(S, nc, nsel)
    full = lambda a: pl.BlockSpec(a.shape, lambda b, i: (0,) * a.ndim)
    return pl.pallas_call(
        functools.partial(_nsa_body, nc=nc, nsel=nsel),
        grid=(B, nT),
        in_specs=[pl.BlockSpec((1, 1, NSA_W, TL), lambda b, i: (b, i, 0, 0)),
                  pl.BlockSpec((1, nc, LANES), lambda b, i: (b, 0, 0)),
                  pl.BlockSpec((1, HEAD_DIM, nc), lambda b, i: (b, 0, 0)),
                  full(ctab), full(aT),
                  pl.BlockSpec((1, S, LANES), lambda b, i: (b, 0, 0)),
                  pl.BlockSpec((1, S, LANES), lambda b, i: (b, 0, 0)),
                  pl.BlockSpec((1, nT, 2 * HEAD_DIM, TL), lambda b, i: (b, 0, 0, 0)),
                  pl.BlockSpec((1, 1, 16, TL), lambda b, i: (b, i, 0, 0)),
                  full(stab), full(wtab)],
        out_specs=pl.BlockSpec((1, 1, NSA_W, TL), lambda b, i: (b, i, 0, 0)),
        out_shape=jax.ShapeDtypeStruct((B, nT, NSA_W, TL), BF16),
        scratch_shapes=[pltpu.VMEM((nsel, TL), F32), pltpu.VMEM((NSA_HEADS, KAUG, TL), BF16)] + _flash_scratch(NSA_HEADS),
        compiler_params=pltpu.CompilerParams(dimension_semantics=("arbitrary",) * 2, vmem_limit_bytes=VMEM_LIMIT),
        name="nsa_attn",
    )(p["nqT"], kcmp, vcmpT, ctab, aT, p["ks"], p["kw"], p["nvT"], p["gT"], stab, wtab)


def _merge_body(x_ref, g_ref, ofT_ref, onT_ref, omT_ref, wfT_ref, wnT_ref, wmT_ref, wgT_ref, bg_ref, woT_ref, out_ref, outT_ref):
    x = x_ref[0]
    hb = _rms(x, g_ref[...]).astype(BF16)
    ys = (_dot(wfT_ref[...], ofT_ref[0, 0]), _dot(wnT_ref[...], onT_ref[0, 0]), _dot(wmT_ref[...], omT_ref[0, 0]))
    merged = jnp.zeros_like(ys[0])
    for r in range(N_BRANCHES):
        rows = slice(r * D_MODEL, (r + 1) * D_MODEL)
        gate = jax.nn.sigmoid(_nt_dot(wgT_ref[rows, :], hb) + bg_ref[rows, :])
        merged = merged + gate * ys[r]
    outT_ref[...] = _dot(woT_ref[...], merged.astype(BF16))
    out_ref[0] = x + outT_ref[...].T


def _merge(x, g, ofT, onT, omT, w_br_fox, w_br_nsa, w_br_moba, w_gate, b_gate, w_out):
    B, S, D = x.shape
    nT = S // TL
    wfT, wnT, wmT = w_br_fox.T.astype(BF16), w_br_nsa.T.astype(BF16), w_br_moba.T.astype(BF16)
    wgT, bg, woT = w_gate.T.astype(BF16), b_gate[:, None], w_out.T.astype(BF16)
    full = lambda a: pl.BlockSpec(a.shape, lambda b, i: (0,) * a.ndim)
    tiled = lambda n: pl.BlockSpec((1, 1, n, TL), lambda b, i: (b, i, 0, 0))
    return pl.pallas_call(
        _merge_body,
        grid=(B, nT),
        in_specs=[pl.BlockSpec((1, TL, D), lambda b, i: (b, i, 0)), full(g), tiled(FOX_W), tiled(NSA_W), tiled(MOBA_W),
                  full(wfT), full(wnT), full(wmT), full(wgT), full(bg), full(woT)],
        out_specs=pl.BlockSpec((1, TL, D), lambda b, i: (b, i, 0)),
        out_shape=jax.ShapeDtypeStruct((B, S, D), F32),
        scratch_shapes=[pltpu.VMEM((D, TL), F32)],
        compiler_params=pltpu.CompilerParams(dimension_semantics=("arbitrary",) * 2, vmem_limit_bytes=VMEM_LIMIT),
        name="merge",
    )(x, g, ofT, onT, omT, wfT, wnT, wmT, wgT, bg, woT)


def _route(lg):
    def softmax_rows(rows):
        m = functools.reduce(jnp.maximum, rows)
        es = [jnp.exp(r - m) for r in rows]
        tot = functools.reduce(lambda a, b: a + b, es)
        return [e / tot for e in es]

    def rank_of(vals, n):
        r = jnp.zeros_like(vals[0])
        for n2 in range(len(vals)):
            if n2 == n:
                continue
            ahead = (vals[n2] > vals[n]) | ((vals[n2] == vals[n]) & (n2 < n))
            r = r + jnp.where(ahead, 1.0, 0.0)
        return r

    gl = [lg[r:r + 1, :] for r in range(N_GROUPS)]
    gp = softmax_rows(gl)
    out = []
    for gi in range(N_GROUPS):
        gw = jnp.where(rank_of(gl, gi) < 1.0, gp[gi], 0.0)
        ep = softmax_rows([lg[8 + gi * EXPERTS_PER_GROUP + e: 9 + gi * EXPERTS_PER_GROUP + e, :]
                           for e in range(EXPERTS_PER_GROUP)])
        top = [jnp.where(rank_of(ep, e) < float(TOPK_IN_GROUP), ep[e], 0.0) for e in range(EXPERTS_PER_GROUP)]
        tot = functools.reduce(lambda a, b: a + b, top)
        out.extend([gw * (t / tot) for t in top])
    return jnp.concatenate(out, axis=0)


def _moe_body(x_ref, g_ref, wr_ref, br_ref, wg_ref, wu_ref, wd_ref, gf_ref, out_ref, hb_ref, comb_ref, *, final):
    e = pl.program_id(1)
    tm = x_ref.shape[0]

    @pl.when(e == 0)
    def _():
        x = x_ref[...]
        h = _rms(x, g_ref[...])
        h_hi = h.astype(BF16)
        h_lo = (h - h_hi.astype(F32)).astype(BF16)
        hb_ref[...] = h_hi
        lg = _dot(h_hi, wr_ref[0]) + _dot(h_lo, wr_ref[0]) + _dot(h_hi, wr_ref[1])
        comb_ref[...] = _route(lg.T[0:32, :] + br_ref[...])
        out_ref[...] = x

    hb = hb_ref[...]
    a = _dot(hb, wg_ref[0])
    u = _dot(hb, wu_ref[0])
    c = jnp.broadcast_to(comb_ref[pl.ds(e, 1), :], (LANES, tm)).T
    hid = (a * jax.nn.sigmoid(a)) * u * jnp.concatenate([c] * (EXPERT_HIDDEN // LANES), axis=1)
    out_ref[...] += _dot(hid.astype(BF16), wd_ref[0])

    if final:
        @pl.when(e == N_EXPERTS - 1)
        def _():
            out_ref[...] = _rms(out_ref[...], gf_ref[...])


def _moe(x, g, w_route_grp, b_route_grp, w_route_exp, b_route_exp, w_exp_gate, w_exp_up, w_exp_down, g_final, final, tm=1024):
    B, S, D = x.shape
    T = B * S
    xt = x.reshape(T, D)
    wr32 = jnp.zeros((D, LANES), F32).at[:, 0:N_GROUPS].set(w_route_grp).at[:, 8:8 + N_EXPERTS].set(w_route_exp)
    wr_hi = wr32.astype(BF16)
    wr = jnp.stack([wr_hi, (wr32 - wr_hi.astype(F32)).astype(BF16)])
    br = jnp.zeros((32, 1), F32).at[0:N_GROUPS, 0].set(b_route_grp).at[8:8 + N_EXPERTS, 0].set(b_route_exp)
    wg, wu, wd = w_exp_gate.astype(BF16), w_exp_up.astype(BF16), w_exp_down.astype(BF16)
    full = lambda a: pl.BlockSpec(a.shape, lambda t, e: (0,) * a.ndim)
    out = pl.pallas_call(
        functools.partial(_moe_body, final=final),
        grid=(T // tm, N_EXPERTS),
        in_specs=[pl.BlockSpec((tm, D), lambda t, e: (t, 0)), full(g), full(wr), full(br),
                  pl.BlockSpec((1, D, EXPERT_HIDDEN), lambda t, e: (e, 0, 0)),
                  pl.BlockSpec((1, D, EXPERT_HIDDEN), lambda t, e: (e, 0, 0)),
                  pl.BlockSpec((1, EXPERT_HIDDEN, D), lambda t, e: (e, 0, 0)),
                  full(g_final)],
        out_specs=pl.BlockSpec((tm, D), lambda t, e: (t, 0)),
        out_shape=jax.ShapeDtypeStruct((T, D), F32),
        scratch_shapes=[pltpu.VMEM((tm, D), BF16), pltpu.VMEM((N_EXPERTS, tm), F32)],
        compiler_params=pltpu.CompilerParams(dimension_semantics=("arbitrary",) * 2, vmem_limit_bytes=VMEM_LIMIT),
        name="moe",
    )(xt, g, wr, br, wg, wu, wd, g_final)
    return out.reshape(B, S, D)


def _fox_placement():
    e = np.zeros((FOX_HEADS, 3, LANES, LANES), np.float32)
    for h in range(FOX_HEADS):
        for j in range(3):
            e[h, j, h, j] = 1.0
    return jnp.asarray(e, BF16)


def kernel(x, norm_mix_g, norm_ffn_g, final_norm_g, w_in, b_in, cmp_w1, cmp_b1, cmp_w2, cmp_pos, w_br_fox, w_br_nsa, w_br_moba, w_out, w_route_grp, b_route_grp, w_route_exp, b_route_exp, w_exp_gate, w_exp_up, w_exp_down):
    depth = w_in.shape[0]
    e_fox = _fox_placement()
    gf = final_norm_g[None, :]
    for l in range(depth):
        g_mix = norm_mix_g[l][None, :]
        ws, bs, wt, bt = _proj_weights(w_in[l], b_in[l])
        p = _project(x, g_mix, ws, bs, wt, bt)
        ofT = _fox_attention(p, e_fox)
        kcmp, vcmpT = _nsa_compress(p, cmp_w1[l], cmp_b1[l], cmp_w2[l], cmp_pos[l])
        onT = _nsa_attention(p, kcmp, vcmpT)
        omT = _moba_attention(p)
        x = _merge(x, g_mix, ofT, onT, omT, w_br_fox[l], w_br_nsa[l], w_br_moba[l],
                   w_in[l][:, _O_ML:_O_END], b_in[l][_O_ML:_O_END], w_out[l])
        x = _moe(x, norm_ffn_g[l][None, :], w_route_grp[l], b_route_grp[l], w_route_exp[l], b_route_exp[l],
                 w_exp_gate[l], w_exp_up[l], w_exp_down[l], gf, final=(l == depth - 1))
    return x
```

```python
import functools

import numpy as np
import jax
import jax.numpy as jnp
from jax import lax
from jax.experimental import pallas as pl
from jax.experimental.pallas import tpu as pltpu

F32 = jnp.float32
BF16 = jnp.bfloat16
HIGHEST = lax.Precision.HIGHEST

D_MODEL = 1024
HEAD_DIM = 64
FOX_HEADS = 6
NSA_HEADS = 4
MOBA_HEADS = 6
FOX_W = FOX_HEADS * HEAD_DIM
NSA_W = NSA_HEADS * HEAD_DIM
MOBA_W = MOBA_HEADS * HEAD_DIM
N_BRANCHES = 3

NSA_CMP_LEN = 32
NSA_CMP_STRIDE = 16
NSA_CMP_HIDDEN = 256
NSA_SEL_LEN = 64
NSA_TOPN = 16
NSA_LOCAL = 2
NSA_WINDOW = 512
MOBA_BLOCK = 256
MOBA_TOPK = 3

N_GROUPS = 4
EXPERTS_PER_GROUP = 4
N_EXPERTS = N_GROUPS * EXPERTS_PER_GROUP
TOPK_IN_GROUP = 2
EXPERT_HIDDEN = 256

RMS_EPS = 1e-6
NEG_INF = -1e30
FORCE_SCORE = 1e9

LANES = 128
TL = 256
KAUG = 2 * LANES
VMEM_LIMIT = 56 * 1024 * 1024

_O_FQ = 0
_O_FK = _O_FQ + FOX_W
_O_FV = _O_FK + FOX_W
_O_FF = _O_FV + FOX_W
_O_NQ = _O_FF + FOX_HEADS
_O_KC = _O_NQ + NSA_W
_O_VC = _O_KC + HEAD_DIM
_O_KS = _O_VC + HEAD_DIM
_O_VS = _O_KS + HEAD_DIM
_O_KW = _O_VS + HEAD_DIM
_O_VW = _O_KW + HEAD_DIM
_O_NG = _O_VW + HEAD_DIM
_O_MQ = _O_NG + N_BRANCHES * NSA_HEADS
_O_MK = _O_MQ + MOBA_W
_O_MV = _O_MK + MOBA_W
_O_ML = _O_MV + MOBA_W
_O_END = _O_ML + N_BRANCHES * D_MODEL

_S_FK = 0
_S_MK = _S_FK + FOX_HEADS * LANES
_S_KS = _S_MK + MOBA_HEADS * LANES
_S_KW = _S_KS + LANES
_S_KC = _S_KW + LANES
_S_VC = _S_KC + LANES
_S_FF = _S_VC + LANES
_S_END = _S_FF + LANES
_T_FQ = 0
_T_FV = _T_FQ + FOX_W
_T_MQ = _T_FV + FOX_W
_T_MV = _T_MQ + MOBA_W
_T_NQ = _T_MV + MOBA_W
_T_NV = _T_NQ + NSA_W
_T_NG = _T_NV + 2 * HEAD_DIM
_T_END = _T_NG + 16

LOG2E = float(np.log2(np.e))
Q_SCALE = HEAD_DIM ** -0.5 * LOG2E


def _alibi_slopes(n):
    return 2.0 ** (-8.0 * np.arange(1, n + 1) / n)


def _split3(v):
    hi = v.astype(BF16)
    r1 = v - hi.astype(F32)
    lo = r1.astype(BF16)
    r2 = r1 - lo.astype(F32)
    return hi, lo, r2.astype(BF16)


def _nt_dot(a, b, precision=None):
    return lax.dot_general(a, b, (((1,), (1,)), ((), ())), preferred_element_type=F32, precision=precision)


def _dot(a, b, precision=None):
    return jnp.dot(a, b, preferred_element_type=F32, precision=precision)


def _rms(x, g):
    return x * lax.rsqrt(jnp.mean(x * x, axis=-1, keepdims=True) + RMS_EPS) * g


def _proj_body(x_ref, g_ref, ws_ref, bs_ref, wt_ref, bt_ref,
               fk_ref, mk_ref, ks_ref, kw_ref, kc_ref, vc_ref, negc_ref, kmean_ref,
               fqT_ref, fvT_ref, mqT_ref, mvT_ref, nqT_ref, nvT_ref, gT_ref, carry_ref, *, tm):
    i = pl.program_id(1)
    hb = _rms(x_ref[0], g_ref[...]).astype(BF16)

    def sdot(a, b):
        return _dot(hb, ws_ref[:, a:b]) + bs_ref[:, a:b]

    def tdot(a, b):
        return _nt_dot(wt_ref[a:b, :], hb) + bt_ref[a:b, :]

    def put_t(ref, val):
        for c in range(tm // TL):
            ref[0, c] = val[:, c * TL:(c + 1) * TL]

    fk_ref[0] = sdot(_S_FK, _S_MK).astype(BF16)
    mk = sdot(_S_MK, _S_KS)
    mk_ref[0] = mk.astype(BF16)
    nblk = tm // MOBA_BLOCK
    means = [jnp.mean(mk[c * MOBA_BLOCK:(c + 1) * MOBA_BLOCK, :], axis=0, keepdims=True) for c in range(nblk)]
    kmean_ref[0, 0] = jnp.concatenate(means + [jnp.zeros((8 - nblk, mk.shape[1]), F32)], axis=0)
    ks_ref[0] = sdot(_S_KS, _S_KW).astype(BF16)
    kw_ref[0] = sdot(_S_KW, _S_KC).astype(BF16)
    kc_ref[0] = sdot(_S_KC, _S_VC)
    vc_ref[0] = sdot(_S_VC, _S_FF)

    @pl.when(i == 0)
    def _():
        carry_ref[...] = jnp.zeros_like(carry_ref)

    fl = sdot(_S_FF, _S_END)
    ls = jnp.minimum(fl, 0.0) - jnp.log1p(jnp.exp(-jnp.abs(fl)))
    ls = jnp.where(lax.broadcasted_iota(jnp.int32, ls.shape, 1) < FOX_HEADS, ls, 0.0)
    tri = (lax.broadcasted_iota(jnp.int32, (tm, tm), 1) <= lax.broadcasted_iota(jnp.int32, (tm, tm), 0)).astype(F32)
    cs = _dot(tri, ls, precision=HIGHEST) + carry_ref[0:1, :]
    negc_ref[0] = -cs
    carry_ref[0:1, :] = cs[tm - 1:tm, :]

    put_t(fqT_ref, tdot(_T_FQ, _T_FV).astype(BF16))
    put_t(fvT_ref, tdot(_T_FV, _T_MQ).astype(BF16))
    put_t(mqT_ref, tdot(_T_MQ, _T_MV).astype(BF16))
    put_t(mvT_ref, tdot(_T_MV, _T_NQ).astype(BF16))
    put_t(nqT_ref, tdot(_T_NQ, _T_NV).astype(BF16))
    put_t(nvT_ref, tdot(_T_NV, _T_NG).astype(BF16))
    put_t(gT_ref, jax.nn.sigmoid(tdot(_T_NG, _T_END)))


def _proj_weights(w, b):
    def pad_heads(cols, nh):
        m = cols.reshape(cols.shape[0], nh, HEAD_DIM)
        return jnp.pad(m, ((0, 0), (0, 0), (0, LANES - HEAD_DIM))).reshape(cols.shape[0], nh * LANES)

    def pad_to(cols, n):
        return jnp.pad(cols, ((0, 0), (0, n - cols.shape[1])))

    wb = jnp.concatenate([w, b[None, :]], axis=0)
    sl = lambda o, n: wb[:, o:o + n]
    std = jnp.concatenate([
        pad_heads(sl(_O_FK, FOX_W), FOX_HEADS), pad_heads(sl(_O_MK, MOBA_W), MOBA_HEADS),
        pad_to(sl(_O_KS, HEAD_DIM), LANES), pad_to(sl(_O_KW, HEAD_DIM), LANES),
        pad_to(sl(_O_KC, HEAD_DIM), LANES), pad_to(sl(_O_VC, HEAD_DIM), LANES),
        pad_to(sl(_O_FF, FOX_HEADS), LANES)], axis=1)
    scale = Q_SCALE
    tr = jnp.concatenate([
        sl(_O_FQ, FOX_W) * scale, sl(_O_FV, FOX_W), sl(_O_MQ, MOBA_W) * scale, sl(_O_MV, MOBA_W),
        sl(_O_NQ, NSA_W) * scale, sl(_O_VS, HEAD_DIM), sl(_O_VW, HEAD_DIM),
        pad_to(sl(_O_NG, N_BRANCHES * NSA_HEADS), 16)], axis=1)
    ws, bs = std[:-1].astype(BF16), std[-1:].astype(F32)
    wt, bt = tr[:-1].T.astype(BF16), tr[-1:].T.astype(F32)
    return ws, bs, wt, bt


def _project(x, g, ws, bs, wt, bt, tm=512):
    B, S, D = x.shape
    nT = S // TL
    nt = tm // TL
    grid = (B, S // tm)
    row = lambda n, dt: jax.ShapeDtypeStruct((B, S, n), dt)
    tiled = lambda n, dt: jax.ShapeDtypeStruct((B, nT, n, TL), dt)
    row_spec = lambda n: pl.BlockSpec((1, tm, n), lambda b, i: (b, i, 0))
    tiled_spec = lambda n: pl.BlockSpec((1, nt, n, TL), lambda b, i: (b, i, 0, 0))
    full = lambda a: pl.BlockSpec(a.shape, lambda b, i: (0,) * a.ndim)
    out_shape = [row(FOX_HEADS * LANES, BF16), row(MOBA_HEADS * LANES, BF16), row(LANES, BF16), row(LANES, BF16),
                 row(LANES, F32), row(LANES, F32), row(LANES, F32),
                 jax.ShapeDtypeStruct((B, S // tm, 8, MOBA_HEADS * LANES), F32),
                 tiled(FOX_W, BF16), tiled(FOX_W, BF16), tiled(MOBA_W, BF16), tiled(MOBA_W, BF16),
                 tiled(NSA_W, BF16), tiled(2 * HEAD_DIM, BF16), tiled(16, F32)]
    out_specs = [row_spec(FOX_HEADS * LANES), row_spec(MOBA_HEADS * LANES), row_spec(LANES), row_spec(LANES),
                 row_spec(LANES), row_spec(LANES), row_spec(LANES),
                 pl.BlockSpec((1, 1, 8, MOBA_HEADS * LANES), lambda b, i: (b, i, 0, 0)),
                 tiled_spec(FOX_W), tiled_spec(FOX_W), tiled_spec(MOBA_W), tiled_spec(MOBA_W),
                 tiled_spec(NSA_W), tiled_spec(2 * HEAD_DIM), tiled_spec(16)]
    outs = pl.pallas_call(
        functools.partial(_proj_body, tm=tm),
        grid=grid,
        in_specs=[pl.BlockSpec((1, tm, D), lambda b, i: (b, i, 0)), full(g), full(ws), full(bs), full(wt), full(bt)],
        out_specs=out_specs,
        out_shape=out_shape,
        scratch_shapes=[pltpu.VMEM((8, LANES), F32)],
        compiler_params=pltpu.CompilerParams(dimension_semantics=("arbitrary", "arbitrary"),
                                             vmem_limit_bytes=VMEM_LIMIT),
        name="proj",
    )(x, g, ws, bs, wt, bt)
    (fk, mk, ks, kw, kc, vc, negc, kmean8, fqT, fvT, mqT, mvT, nqT, nvT, gT) = outs
    nblk = tm // MOBA_BLOCK
    kmean = kmean8[:, :, :nblk, :].reshape(B, (S // tm) * nblk, MOBA_HEADS * LANES)
    return dict(fk=fk, mk=mk, ks=ks, kw=kw, kc=kc, vc=vc, negc=negc, kmean=kmean,
                fqT=fqT, fvT=fvT, mqT=mqT, mvT=mvT, nqT=nqT, nvT=nvT, gT=gT)


def _flash_sweep(n_heads, diag_tile, n_rest, rest_tile, scores, value_tile, bufs, m_ref, l_ref, acc_ref):
    half = n_heads // 2
    groups = (tuple(range(half)), tuple(range(half, n_heads)))
    for h in range(n_heads):
        m_ref[h] = jnp.full(m_ref.shape[1:], NEG_INF, F32)
        l_ref[h] = jnp.zeros(l_ref.shape[1:], F32)
        acc_ref[h] = jnp.zeros(acc_ref.shape[1:], F32)

    def produce(j, grp, t):
        mts = []
        for h in grp:
            sT = scores(j, h, t)
            bufs[h][...] = sT
            mts.append(jnp.max(sT, axis=0, keepdims=True))
        return tuple(mts)

    def consume(j, grp, mts):
        for h, mt in zip(grp, mts):
            m = m_ref[h, 0:1, :]
            m_new = jnp.maximum(m, mt)
            p = jnp.exp2(bufs[h][...] - m_new)
            alpha = jnp.exp2(m - m_new)
            m_ref[h, 0:1, :] = m_new
            l_ref[h, 0:1, :] = alpha * l_ref[h, 0:1, :] + jnp.sum(p, axis=0, keepdims=True)
            acc_ref[h] = alpha * acc_ref[h] + _dot(value_tile(j, h), p.astype(BF16))

    def tile_steps(j, t, prev):
        mts0 = produce(j, groups[0], t)
        if prev is not None:
            consume(prev[0], groups[1], prev[1])
        mts1 = produce(j, groups[1], t)
        consume(j, groups[0], mts0)
        return mts1

    def step(t, mts):
        prev_tile = jnp.where(t == 0, diag_tile, rest_tile(t - 1))
        return tile_steps(rest_tile(t), t, (prev_tile, mts))

    mts = lax.fori_loop(0, n_rest // 2, lambda u, mts: step(2 * u + 1, step(2 * u, mts)),
                        tile_steps(diag_tile, None, None))
    mts = lax.cond(n_rest % 2 == 1, lambda mts: step(n_rest - 1, mts), lambda mts: mts, mts)
    consume(jnp.where(n_rest == 0, diag_tile, rest_tile(n_rest - 1)), groups[1], mts)


def _tile_iotas(tk, tq):
    kk = lax.broadcasted_iota(jnp.int32, (tk, tq), 0)
    qq = lax.broadcasted_iota(jnp.int32, (tk, tq), 1)
    return kk, qq


def _feature_rows(nrows, tq, value):
    r = lax.broadcasted_iota(jnp.int32, (16, tq), 0)
    return jnp.where(r < nrows, value, 0.0).astype(F32)


def _causal_keep(tq):
    kk, qq = _tile_iotas(TL, tq)
    return kk <= qq


def _flash_scratch(n_heads):
    return ([pltpu.VMEM((TL, TL), F32) for _ in range(n_heads)]
            + [pltpu.VMEM((n_heads, 8, TL), F32), pltpu.VMEM((n_heads, 8, TL), F32),
               pltpu.VMEM((n_heads, HEAD_DIM, TL), F32)])


def _fox_body(qT_ref, k_ref, vT_ref, negc_ref, e_ref, o_ref, kfeat_ref, q2_ref, *flash, S, hps):
    bufs, (m_ref, l_ref, acc_ref) = flash[:hps], flash[hps:]
    i = pl.program_id(2)
    tq = TL

    @pl.when(i == 0)
    def _():
        chunk = 512
        for c in range(S // chunk):
            hi, lo, lo2 = _split3(negc_ref[0, c * chunk:(c + 1) * chunk, :] * LOG2E)
            for h in range(hps):
                feat = _dot(hi, e_ref[h, 0]) + _dot(lo, e_ref[h, 1]) + _dot(lo2, e_ref[h, 2])
                kfeat_ref[h, c * chunk:(c + 1) * chunk, :] = feat.astype(BF16)

    feat_rows = jnp.concatenate([_feature_rows(3, tq, 1.0), jnp.zeros((LANES - 16, tq), F32)], axis=0).astype(BF16)
    for h in range(hps):
        q2_ref[h] = jnp.concatenate([qT_ref[0, 0, h * HEAD_DIM:(h + 1) * HEAD_DIM, :],
                                     jnp.zeros((HEAD_DIM, tq), BF16), feat_rows], axis=0)

    def scores(j, h, t):
        k2 = jnp.concatenate([k_ref[0, pl.ds(j * TL, TL), h * LANES:(h + 1) * LANES],
                              kfeat_ref[h, pl.ds(j * TL, TL), :]], axis=1)
        sT = _dot(k2, q2_ref[h])
        return jnp.where(_causal_keep(tq), sT, NEG_INF) if t is None else sT

    _flash_sweep(hps, i, i, lambda t: t, scores, lambda j, h: vT_ref[0, j, h * HEAD_DIM:(h + 1) * HEAD_DIM, :],
                 bufs, m_ref, l_ref, acc_ref)
    for h in range(hps):
        o_ref[0, 0, h * HEAD_DIM:(h + 1) * HEAD_DIM, :] = (acc_ref[h] / l_ref[h, 0:1, :]).astype(BF16)


def _fox_attention(p, e_fox, hps=FOX_HEADS):
    B, nT = p["fqT"].shape[:2]
    S = nT * TL
    return pl.pallas_call(
        functools.partial(_fox_body, S=S, hps=hps),
        grid=(B, FOX_HEADS // hps, nT),
        in_specs=[pl.BlockSpec((1, 1, hps * HEAD_DIM, TL), lambda b, h, i: (b, i, h, 0)),
                  pl.BlockSpec((1, S, hps * LANES), lambda b, h, i: (b, 0, h)),
                  pl.BlockSpec((1, nT, hps * HEAD_DIM, TL), lambda b, h, i: (b, 0, h, 0)),
                  pl.BlockSpec((1, S, LANES), lambda b, h, i: (b, 0, 0)),
                  pl.BlockSpec((hps, 3, LANES, LANES), lambda b, h, i: (h, 0, 0, 0))],
        out_specs=pl.BlockSpec((1, 1, hps * HEAD_DIM, TL), lambda b, h, i: (b, i, h, 0)),
        out_shape=jax.ShapeDtypeStruct((B, nT, FOX_W, TL), BF16),
        scratch_shapes=[pltpu.VMEM((hps, S, LANES), BF16), pltpu.VMEM((hps, KAUG, TL), BF16)] + _flash_scratch(hps),
        compiler_params=pltpu.CompilerParams(dimension_semantics=("arbitrary",) * 3, vmem_limit_bytes=VMEM_LIMIT),
        name="fox_attn",
    )(p["fqT"], p["fk"], p["fvT"], p["negc"], e_fox)


def _rank_rows(score, nrows):
    ridx = lax.broadcasted_iota(jnp.int32, score.shape, 0)
    rank = jnp.zeros(score.shape, F32)
    for n2 in range(nrows):
        row = score[n2:n2 + 1, :]
        ahead = (row > score) | ((row == score) & (n2 < ridx))
        rank = rank + jnp.where(ahead, 1.0, 0.0)
    return rank


def _moba_body(qT_ref, k_ref, vT_ref, kmean_ref, ptab_ref, o_ref, q2_ref, *flash, nbp, hps):
    bufs, (m_ref, l_ref, acc_ref) = flash[:hps], flash[hps:]
    i = pl.program_id(2)
    tq = TL
    zpad = jnp.zeros((HEAD_DIM, tq), BF16)
    blk = lax.broadcasted_iota(jnp.int32, (nbp, tq), 0)
    for h in range(hps):
        qT = qT_ref[0, 0, h * HEAD_DIM:(h + 1) * HEAD_DIM, :]
        gate = _dot(kmean_ref[0, :, h * LANES:(h + 1) * LANES].astype(BF16),
                    jnp.concatenate([qT, zpad], axis=0)) * (1.0 / Q_SCALE)
        gate = jnp.where(blk < i, gate, -FORCE_SCORE)
        sel = (_rank_rows(gate, nbp) < MOBA_TOPK) & (blk < i)
        selbias = jnp.where(sel | (blk == i), 0.0, NEG_INF)
        feat_rows = jnp.concatenate([selbias, _feature_rows(3, tq, 1.0), jnp.zeros((LANES - nbp - 16, tq), F32)], axis=0)
        q2_ref[h] = jnp.concatenate([qT, zpad, feat_rows.astype(BF16)], axis=0)

    def scores(j, h, t):
        k2 = jnp.concatenate([k_ref[0, pl.ds(j * TL, TL), h * LANES:(h + 1) * LANES],
                              ptab_ref[h, pl.ds(j * TL, TL), :]], axis=1)
        sT = _dot(k2, q2_ref[h])
        return jnp.where(_causal_keep(tq), sT, NEG_INF) if t is None else sT

    _flash_sweep(hps, i, i, lambda t: t, scores, lambda j, h: vT_ref[0, j, h * HEAD_DIM:(h + 1) * HEAD_DIM, :],
                 bufs, m_ref, l_ref, acc_ref)
    for h in range(hps):
        o_ref[0, 0, h * HEAD_DIM:(h + 1) * HEAD_DIM, :] = (acc_ref[h] / l_ref[h, 0:1, :]).astype(BF16)


def _moba_tables(S, nbp):
    pos = np.arange(S)
    onehot = (pos[:, None] // MOBA_BLOCK == np.arange(nbp)[None, :]).astype(np.float32)
    tabs = []
    for slope in _alibi_slopes(MOBA_HEADS):
        hi, lo, lo2 = _split3(jnp.asarray(slope * LOG2E * pos, F32))
        feat = jnp.stack([hi, lo, lo2], axis=1).astype(F32)
        tabs.append(jnp.concatenate([jnp.asarray(onehot), feat, jnp.zeros((S, LANES - nbp - 3), F32)], axis=1))
    return jnp.stack(tabs).astype(BF16)


def _moba_attention(p, hps=MOBA_HEADS):
    B, nT = p["mqT"].shape[:2]
    S = nT * TL
    nbp = -(-nT // 16) * 16
    kmean = jnp.pad(p["kmean"], ((0, 0), (0, nbp - nT), (0, 0)))
    ptab = _moba_tables(S, nbp)
    return pl.pallas_call(
        functools.partial(_moba_body, nbp=nbp, hps=hps),
        grid=(B, MOBA_HEADS // hps, nT),
        in_specs=[pl.BlockSpec((1, 1, hps * HEAD_DIM, TL), lambda b, h, i: (b, i, h, 0)),
                  pl.BlockSpec((1, S, hps * LANES), lambda b, h, i: (b, 0, h)),
                  pl.BlockSpec((1, nT, hps * HEAD_DIM, TL), lambda b, h, i: (b, 0, h, 0)),
                  pl.BlockSpec((1, nbp, hps * LANES), lambda b, h, i: (b, 0, h)),
                  pl.BlockSpec((hps, S, LANES), lambda b, h, i: (h, 0, 0))],
        out_specs=pl.BlockSpec((1, 1, hps * HEAD_DIM, TL), lambda b, h, i: (b, i, h, 0)),
        out_shape=jax.ShapeDtypeStruct((B, nT, MOBA_W, TL), BF16),
        scratch_shapes=[pltpu.VMEM((hps, KAUG, TL), BF16)] + _flash_scratch(hps),
        compiler_params=pltpu.CompilerParams(dimension_semantics=("arbitrary",) * 3, vmem_limit_bytes=VMEM_LIMIT),
        name="moba_attn",
    )(p["mqT"], p["mk"], p["mvT"], kmean, ptab)


def _cmp_body(kc_ref, vc_ref, w1_ref, b1_ref, w2k_ref, w2vT_ref, pos_ref, kcmp_ref, vcmpT_ref, pad_ref, *, S, nc):
    tail = pad_ref.shape[0] - S

    def hidden(src_ref, which):
        pad_ref[0:S, :] = src_ref[0]
        pad_ref[S:S + tail, :] = jnp.zeros((tail, LANES), F32)
        acc = jnp.zeros((nc, NSA_CMP_HIDDEN), F32)
        for l in range(NSA_CMP_LEN):
            rows = pad_ref[pl.ds(l, nc, stride=NSA_CMP_STRIDE), :] + pos_ref[which, l:l + 1, :]
            acc = acc + _dot(rows.astype(BF16), w1_ref[which, l])
        return jax.nn.gelu(acc + b1_ref[which]).astype(BF16)

    kcmp_ref[0] = _dot(hidden(kc_ref, 0), w2k_ref[...]).astype(BF16)
    vcmpT_ref[0] = _nt_dot(w2vT_ref[...], hidden(vc_ref, 1)).astype(BF16)


def _nsa_compress(p, cmp_w1, cmp_b1, cmp_w2, cmp_pos):
    B, S, _ = p["kc"].shape
    nc = S // NSA_CMP_STRIDE
    w1 = jnp.pad(cmp_w1, ((0, 0), (0, 0), (0, LANES - HEAD_DIM), (0, 0))).astype(BF16)
    pos = jnp.pad(cmp_pos, ((0, 0), (0, 0), (0, LANES - HEAD_DIM)))
    b1 = cmp_b1[:, None, :]
    w2k = jnp.pad(cmp_w2[0], ((0, 0), (0, LANES - HEAD_DIM))).astype(BF16)
    w2vT = cmp_w2[1].T.astype(BF16)
    full = lambda a: pl.BlockSpec(a.shape, lambda b: (0,) * a.ndim)
    return pl.pallas_call(
        functools.partial(_cmp_body, S=S, nc=nc),
        grid=(B,),
        in_specs=[pl.BlockSpec((1, S, LANES), lambda b: (b, 0, 0)), pl.BlockSpec((1, S, LANES), lambda b: (b, 0, 0)),
                  full(w1), full(b1), full(w2k), full(w2vT), full(pos)],
        out_specs=[pl.BlockSpec((1, nc, LANES), lambda b: (b, 0, 0)), pl.BlockSpec((1, HEAD_DIM, nc), lambda b: (b, 0, 0))],
        out_shape=[jax.ShapeDtypeStruct((B, nc, LANES), BF16), jax.ShapeDtypeStruct((B, HEAD_DIM, nc), BF16)],
        scratch_shapes=[pltpu.VMEM((S + NSA_CMP_LEN, LANES), F32)],
        compiler_params=pltpu.CompilerParams(dimension_semantics=("arbitrary",), vmem_limit_bytes=VMEM_LIMIT),
        name="nsa_compress",
    )(p["kc"], p["vc"], w1, b1, w2k, w2vT, pos)


def _nsa_body(qT_ref, kcmp_ref, vcmpT_ref, ctab_ref, aT_ref, ks_ref, kw_ref, vT_ref, gT_ref, stab_ref, wtab_ref,
              o_ref, score_ref, q2_ref, *flash, nc, nsel):
    bufs, (m_ref, l_ref, acc_ref) = flash[:NSA_HEADS], flash[NSA_HEADS:]
    i = pl.program_id(1)
    tq = TL
    t0 = i * tq
    slopes = _alibi_slopes(NSA_HEADS)
    zpad = jnp.zeros((HEAD_DIM, tq), BF16)

    def q2T_of(h, selbias):
        rows = jnp.concatenate([selbias, _feature_rows(3, tq, float(slopes[h])),
                                jnp.zeros((LANES - nsel - 16, tq), F32)], axis=0)
        return jnp.concatenate([qT_ref[0, 0, h * HEAD_DIM:(h + 1) * HEAD_DIM, :], zpad, rows.astype(BF16)], axis=0)

    kc2 = jnp.concatenate([kcmp_ref[0], ctab_ref[...]], axis=1)
    nn, qq = _tile_iotas(nc, tq)
    vis = nn * NSA_CMP_STRIDE + (NSA_CMP_LEN - 1) <= t0 + qq
    nosel = jnp.zeros((nsel, tq), F32)
    m_c = []
    for h in range(NSA_HEADS):
        sT = jnp.where(vis, _dot(kc2, q2T_of(h, nosel)), NEG_INF)
        bufs[h][0:nc, :] = sT
        m_c.append(jnp.max(sT, axis=0, keepdims=True))
    imp = jnp.zeros((nc, tq), F32)
    o_c = []
    for h in range(NSA_HEADS):
        e = jnp.where(vis, jnp.exp2(bufs[h][0:nc, :] - m_c[h]), 0.0)
        l = jnp.sum(e, axis=0, keepdims=True)
        pc = e * (1.0 / jnp.where(l > 0.0, l, 1.0))
        imp = imp + pc
        o_c.append(_dot(vcmpT_ref[0], pc.astype(BF16)))

    imp_sel = _dot(aT_ref[...], imp, precision=HIGHEST)
    jj, qs = _tile_iotas(nsel, tq)
    cur = (t0 + qs) // NSA_SEL_LEN
    causal = jj <= cur
    forced = (jj == 0) | (jj > cur - NSA_LOCAL)
    score = jnp.where(causal, jnp.where(forced, FORCE_SCORE, imp_sel), -FORCE_SCORE)
    score_ref[...] = score

    def rank_step(j2, rank):
        row = score_ref[pl.ds(j2, 1), :]
        ahead = (row > score) | ((row == score) & (j2 < jj))
        return rank + jnp.where(ahead, 1.0, 0.0)

    def rank_chunk(c, rank):
        for r in range(8):
            rank = rank_step(c * 8 + r, rank)
        return rank

    rows_needed = (i + 1) * (TL // NSA_SEL_LEN)
    rank = lax.fori_loop(0, jnp.minimum((rows_needed + 7) // 8, nsel // 8), rank_chunk, jnp.zeros((nsel, tq), F32))
    selbias = jnp.where(rank < min(NSA_TOPN, nsel), 0.0, NEG_INF)

    kk, qk = _tile_iotas(TL, tq)

    def branch_scores(k_ref, tab_ref, oldest):
        def scores(j, h, t):
            k2 = jnp.concatenate([k_ref[0, pl.ds(j * TL, TL), :], tab_ref[pl.ds(j * TL, TL), :]], axis=1)
            sT = _dot(k2, q2_ref[h])
            if t is None:
                return jnp.where(kk <= qk, sT, NEG_INF)
            if oldest is not None:
                return jnp.where((t != oldest) | (kk > qk), sT, NEG_INF)
            return sT
        return scores

    for h in range(NSA_HEADS):
        q2_ref[h] = q2T_of(h, selbias)
    _flash_sweep(NSA_HEADS, i, i, lambda t: t, branch_scores(ks_ref, stab_ref, None),
                 lambda j, h: vT_ref[0, j, 0:HEAD_DIM, :], bufs, m_ref, l_ref, acc_ref)
    o_s = [acc_ref[h] / l_ref[h, 0:1, :] for h in range(NSA_HEADS)]

    for h in range(NSA_HEADS):
        q2_ref[h] = q2T_of(h, nosel)
    nprev = NSA_WINDOW // TL
    _flash_sweep(NSA_HEADS, i, jnp.minimum(i, nprev), lambda t: i - 1 - t,
                 branch_scores(kw_ref, wtab_ref, nprev - 1),
                 lambda j, h: vT_ref[0, j, HEAD_DIM:2 * HEAD_DIM, :], bufs, m_ref, l_ref, acc_ref)

    for h in range(NSA_HEADS):
        o_w = acc_ref[h] / l_ref[h, 0:1, :]
        g = gT_ref[0, 0]
        out = (g[h:h + 1, :] * o_c[h] + g[NSA_HEADS + h:NSA_HEADS + h + 1, :] * o_s[h]
               + g[2 * NSA_HEADS + h:2 * NSA_HEADS + h + 1, :] * o_w)
        o_ref[0, 0, h * HEAD_DIM:(h + 1) * HEAD_DIM, :] = out.astype(BF16)


def _nsa_tables(S, nc, nsel):
    pos = np.arange(S)
    onehot = jnp.asarray((pos[:, None] // NSA_SEL_LEN == np.arange(nsel)[None, :]).astype(np.float32))
    pfeat = jnp.stack(_split3(jnp.asarray(pos * LOG2E, F32)), axis=1).astype(F32)
    rest = jnp.zeros((S, LANES - nsel - 3), F32)
    stab = jnp.concatenate([onehot, pfeat, rest], axis=1).astype(BF16)
    wtab = jnp.concatenate([jnp.zeros_like(onehot), pfeat, rest], axis=1).astype(BF16)
    cend = np.arange(nc) * NSA_CMP_STRIDE + NSA_CMP_LEN - 1
    cfeat = jnp.stack(_split3(jnp.asarray(cend * LOG2E, F32)), axis=1).astype(F32)
    ctab = jnp.concatenate([jnp.zeros((nc, nsel), F32), cfeat, jnp.zeros((nc, LANES - nsel - 3), F32)], axis=1).astype(BF16)
    ratio = NSA_SEL_LEN // NSA_CMP_STRIDE
    span = NSA_CMP_LEN // NSA_CMP_STRIDE
    a = np.zeros((nsel, nc), np.float32)
    for j in range(nsel):
        for mm in range(ratio):
            for n2 in range(span):
                if j * ratio + mm + n2 < nc:
                    a[j, j * ratio + mm + n2] += 1.0
    return stab, wtab, ctab, jnp.asarray(a)


def _nsa_attention(p, kcmp, vcmpT):
    B, nT = p["nqT"].shape[:2]
    S = nT * TL
    nc = kcmp.shape[1]
    nsel = S // NSA_SEL_LEN
    stab, wtab, ctab, aT = _nsa_tables(S, nc, nsel)
    full = lambda a: pl.BlockSpec(a.shape, lambda b, i: (0,) * a.ndim)
    return pl.pallas_call(
        functools.partial(_nsa_body, nc=nc, nsel=nsel),
        grid=(B, nT),
        in_specs=[pl.BlockSpec((1, 1, NSA_W, TL), lambda b, i: (b, i, 0, 0)),
                  pl.BlockSpec((1, nc, LANES), lambda b, i: (b, 0, 0)),
                  pl.BlockSpec((1, HEAD_DIM, nc), lambda b, i: (b, 0, 0)),
                  full(ctab), full(aT),
                  pl.BlockSpec((1, S, LANES), lambda b, i: (b, 0, 0)),
                  pl.BlockSpec((1, S, LANES), lambda b, i: (b, 0, 0)),
                  pl.BlockSpec((1, nT, 2 * HEAD_DIM, TL), lambda b, i: (b, 0, 0, 0)),
                  pl.BlockSpec((1, 1, 16, TL), lambda b, i: (b, i, 0, 0)),
                  full(stab), full(wtab)],
        out_specs=pl.BlockSpec((1, 1, NSA_W, TL), lambda b, i: (b, i, 0, 0)),
        out_shape=jax.ShapeDtypeStruct((B, nT, NSA_W, TL), BF16),
        scratch_shapes=[pltpu.VMEM((nsel, TL), F32), pltpu.VMEM((NSA_HEADS, KAUG, TL), BF16)] + _flash_scratch(NSA_HEADS),
        compiler_params=pltpu.CompilerParams(dimension_semantics=("arbitrary",) * 2, vmem_limit_bytes=VMEM_LIMIT),
        name="nsa_attn",
    )(p["nqT"], kcmp, vcmpT, ctab, aT, p["ks"], p["kw"], p["nvT"], p["gT"], stab, wtab)


def _merge_body(x_ref, g_ref, ofT_ref, onT_ref, omT_ref, wfT_ref, wnT_ref, wmT_ref, wgT_ref, bg_ref, woT_ref, out_ref, outT_ref):
    x = x_ref[0]
    hb = _rms(x, g_ref[...]).astype(BF16)
    ys = (_dot(wfT_ref[...], ofT_ref[0, 0]), _dot(wnT_ref[...], onT_ref[0, 0]), _dot(wmT_ref[...], omT_ref[0, 0]))
    merged = jnp.zeros_like(ys[0])
    for r in range(N_BRANCHES):
        rows = slice(r * D_MODEL, (r + 1) * D_MODEL)
        gate = jax.nn.sigmoid(_nt_dot(wgT_ref[rows, :], hb) + bg_ref[rows, :])
        merged = merged + gate * ys[r]
    outT_ref[...] = _dot(woT_ref[...], merged.astype(BF16))
    out_ref[0] = x + outT_ref[...].T


def _merge(x, g, ofT, onT, omT, w_br_fox, w_br_nsa, w_br_moba, w_gate, b_gate, w_out):
    B, S, D = x.shape
    nT = S // TL
    wfT, wnT, wmT = w_br_fox.T.astype(BF16), w_br_nsa.T.astype(BF16), w_br_moba.T.astype(BF16)
    wgT, bg, woT = w_gate.T.astype(BF16), b_gate[:, None], w_out.T.astype(BF16)
    full = lambda a: pl.BlockSpec(a.shape, lambda b, i: (0,) * a.ndim)
    tiled = lambda n: pl.BlockSpec((1, 1, n, TL), lambda b, i: (b, i, 0, 0))
    return pl.pallas_call(
        _merge_body,
        grid=(B, nT),
        in_specs=[pl.BlockSpec((1, TL, D), lambda b, i: (b, i, 0)), full(g), tiled(FOX_W), tiled(NSA_W), tiled(MOBA_W),
                  full(wfT), full(wnT), full(wmT), full(wgT), full(bg), full(woT)],
        out_specs=pl.BlockSpec((1, TL, D), lambda b, i: (b, i, 0)),
        out_shape=jax.ShapeDtypeStruct((B, S, D), F32),
        scratch_shapes=[pltpu.VMEM((D, TL), F32)],
        compiler_params=pltpu.CompilerParams(dimension_semantics=("arbitrary",) * 2, vmem_limit_bytes=VMEM_LIMIT),
        name="merge",
    )(x, g, ofT, onT, omT, wfT, wnT, wmT, wgT, bg, woT)


def _route(lg):
    def softmax_rows(rows):
        m = functools.reduce(jnp.maximum, rows)
        es = [jnp.exp(r - m) for r in rows]
        tot = functools.reduce(lambda a, b: a + b, es)
        return [e / tot for e in es]

    def rank_of(vals, n):
        r = jnp.zeros_like(vals[0])
        for n2 in range(len(vals)):
            if n2 == n:
                continue
            ahead = (vals[n2] > vals[n]) | ((vals[n2] == vals[n]) & (n2 < n))
            r = r + jnp.where(ahead, 1.0, 0.0)
        return r

    gl = [lg[r:r + 1, :] for r in range(N_GROUPS)]
    gp = softmax_rows(gl)
    out = []
    for gi in range(N_GROUPS):
        gw = jnp.where(rank_of(gl, gi) < 1.0, gp[gi], 0.0)
        ep = softmax_rows([lg[8 + gi * EXPERTS_PER_GROUP + e: 9 + gi * EXPERTS_PER_GROUP + e, :]
                           for e in range(EXPERTS_PER_GROUP)])
        top = [jnp.where(rank_of(ep, e) < float(TOPK_IN_GROUP), ep[e], 0.0) for e in range(EXPERTS_PER_GROUP)]
        tot = functools.reduce(lambda a, b: a + b, top)
        out.extend([gw * (t / tot) for t in top])
    return jnp.concatenate(out, axis=0)


def _moe_body(x_ref, g_ref, wr_ref, br_ref, wg_ref, wu_ref, wd_ref, gf_ref, out_ref, hb_ref, comb_ref, *, final):
    e = pl.program_id(1)
    tm = x_ref.shape[0]

    @pl.when(e == 0)
    def _():
        x = x_ref[...]
        h = _rms(x, g_ref[...])
        h_hi = h.astype(BF16)
        h_lo = (h - h_hi.astype(F32)).astype(BF16)
        hb_ref[...] = h_hi
        lg = _dot(h_hi, wr_ref[0]) + _dot(h_lo, wr_ref[0]) + _dot(h_hi, wr_ref[1])
        comb_ref[...] = _route(lg.T[0:32, :] + br_ref[...])
        out_ref[...] = x

    hb = hb_ref[...]
    a = _dot(hb, wg_ref[0])
    u = _dot(hb, wu_ref[0])
    c = jnp.broadcast_to(comb_ref[pl.ds(e, 1), :], (LANES, tm)).T
    hid = (a * jax.nn.sigmoid(a)) * u * jnp.concatenate([c] * (EXPERT_HIDDEN // LANES), axis=1)
    out_ref[...] += _dot(hid.astype(BF16), wd_ref[0])

    if final:
        @pl.when(e == N_EXPERTS - 1)
        def _():
            out_ref[...] = _rms(out_ref[...], gf_ref[...])


def _moe(x, g, w_route_grp, b_route_grp, w_route_exp, b_route_exp, w_exp_gate, w_exp_up, w_exp_down, g_final, final, tm=1024):
    B, S, D = x.shape
    T = B * S
    xt = x.reshape(T, D)
    wr32 = jnp.zeros((D, LANES), F32).at[:, 0:N_GROUPS].set(w_route_grp).at[:, 8:8 + N_EXPERTS].set(w_route_exp)
    wr_hi = wr32.astype(BF16)
    wr = jnp.stack([wr_hi, (wr32 - wr_hi.astype(F32)).astype(BF16)])
    br = jnp.zeros((32, 1), F32).at[0:N_GROUPS, 0].set(b_route_grp).at[8:8 + N_EXPERTS, 0].set(b_route_exp)
    wg, wu, wd = w_exp_gate.astype(BF16), w_exp_up.astype(BF16), w_exp_down.astype(BF16)
    full = lambda a: pl.BlockSpec(a.shape, lambda t, e: (0,) * a.ndim)
    out = pl.pallas_call(
        functools.partial(_moe_body, final=final),
        grid=(T // tm, N_EXPERTS),
        in_specs=[pl.BlockSpec((tm, D), lambda t, e: (t, 0)), full(g), full(wr), full(br),
                  pl.BlockSpec((1, D, EXPERT_HIDDEN), lambda t, e: (e, 0, 0)),
                  pl.BlockSpec((1, D, EXPERT_HIDDEN), lambda t, e: (e, 0, 0)),
                  pl.BlockSpec((1, EXPERT_HIDDEN, D), lambda t, e: (e, 0, 0)),
                  full(g_final)],
        out_specs=pl.BlockSpec((tm, D), lambda t, e: (t, 0)),
        out_shape=jax.ShapeDtypeStruct((T, D), F32),
        scratch_shapes=[pltpu.VMEM((tm, D), BF16), pltpu.VMEM((N_EXPERTS, tm), F32)],
        compiler_params=pltpu.CompilerParams(dimension_semantics=("arbitrary",) * 2, vmem_limit_bytes=VMEM_LIMIT),
        name="moe",
    )(xt, g, wr, br, wg, wu, wd, g_final)
    return out.reshape(B, S, D)


def _fox_placement():
    e = np.zeros((FOX_HEADS, 3, LANES, LANES), np.float32)
    for h in range(FOX_HEADS):
        for j in range(3):
            e[h, j, h, j] = 1.0
    return jnp.asarray(e, BF16)


def kernel(x, norm_mix_g, norm_ffn_g, final_norm_g, w_in, b_in, cmp_w1, cmp_b1, cmp_w2, cmp_pos, w_br_fox, w_br_nsa, w_br_moba, w_out, w_route_grp, b_route_grp, w_route_exp, b_route_exp, w_exp_gate, w_exp_up, w_exp_down):
    depth = w_in.shape[0]
    e_fox = _fox_placement()
    gf = final_norm_g[None, :]
    for l in range(depth):
        g_mix = norm_mix_g[l][None, :]
        ws, bs, wt, bt = _proj_weights(w_in[l], b_in[l])
        p = _project(x, g_mix, ws, bs, wt, bt)
        ofT = _fox_attention(p, e_fox)
        kcmp, vcmpT = _nsa_compress(p, cmp_w1[l], cmp_b1[l], cmp_w2[l], cmp_pos[l])
        onT = _nsa_attention(p, kcmp, vcmpT)
        omT = _moba_attention(p)
        x = _merge(x, g_mix, ofT, onT, omT, w_br_fox[l], w_br_nsa[l], w_br_moba[l],
                   w_in[l][:, _O_ML:_O_END], b_in[l][_O_ML:_O_END], w_out[l])
        x = _moe(x, norm_ffn_g[l][None, :], w_route_grp[l], b_route_grp[l], w_route_exp[l], b_route_exp[l],
                 w_exp_gate[l], w_exp_up[l], w_exp_down[l], gf, final=(l == depth - 1))
    return x
```
